```python
import jax, jax.numpy as jnp
from jax import lax
import numpy as np

D_MODEL = 1024
BATCH = 8
SEQ = 2048
DEPTH = 4
DEC_BATCH = 128
DEC_SEQ = 4
PAST_LEN = 16384
PAGE_SIZE = 128

N_MIXERS = 2
N_GDN = (DEPTH + 1) // 2
N_RWKV = DEPTH // 2
GDN_HEADS = 8
GDN_HEAD_DIM = 128
GDN_QK = GDN_HEADS * GDN_HEAD_DIM
GDN_V = GDN_HEADS * GDN_HEAD_DIM
GDN_CONV_DIM = 2 * GDN_QK + GDN_V
GDN_IN = GDN_CONV_DIM + GDN_V + 2 * GDN_HEADS
GDN_CONV = 4
GDN_CHUNK = 64
RWKV_HEAD = 64
RWKV_HEADS = D_MODEL // RWKV_HEAD
D_DECAY_LORA = 64
D_AAA_LORA = 64
D_MV_LORA = 32
D_GATE_LORA = 128
D_FF = 2816
N_NORMS = 6
NORM_EPS = 1e-6
L2_EPS = 1e-6
GN_EPS = 64e-5

kernel_name = 'hybrid_gdn_rwkv7_macaron_step'


def rms_norm(x, g):
    xf = x.astype(jnp.float32)
    y = xf * lax.rsqrt(jnp.mean(xf * xf, axis=-1, keepdims=True) + NORM_EPS)
    return (y * g.astype(jnp.float32)).astype(x.dtype)


def l2_normalize(x):
    xf = x.astype(jnp.float32)
    return xf * lax.rsqrt(jnp.sum(xf * xf, axis=-1, keepdims=True) + L2_EPS)


def swiglu(x, w_in, w_out):
    gate, up = jnp.split(x @ w_in, 2, axis=-1)
    return (jax.nn.silu(gate) * up) @ w_out


def causal_dwconv(x_ext, w):
    return lax.conv_general_dilated(
        x_ext, w[:, None, :].astype(x_ext.dtype), window_strides=(1,), padding='VALID',
        dimension_numbers=('NWC', 'WIO', 'NWC'), feature_group_count=x_ext.shape[-1])


def chunk_gated_delta(q, k, v, g, beta, s0):
    B, T, H, dk = q.shape
    dv = v.shape[-1]
    C = min(GDN_CHUNK, T)
    pad = (-T) % C
    n = (T + pad) // C

    def prep(t):
        t = jnp.moveaxis(t.astype(jnp.float32), 1, 2)
        t = jnp.pad(t, [(0, 0), (0, 0), (0, pad)] + [(0, 0)] * (t.ndim - 3))
        return t.reshape(B, H, n, C, *t.shape[3:])

    q = prep(l2_normalize(q) * dk ** -0.5)
    k = prep(l2_normalize(k))
    v = prep(v)
    beta = prep(beta)
    g = jnp.cumsum(prep(g), axis=-1)
    idx = jnp.arange(C)
    incl = idx[:, None] >= idx[None, :]
    strict = idx[:, None] > idx[None, :]
    decay = jnp.exp(jnp.where(incl, g[..., :, None] - g[..., None, :], -jnp.inf))
    kb = k * beta[..., None]
    a_mat = jnp.where(strict, jnp.einsum('bhncd,bhnsd->bhncs', kb, k) * decay, 0.0) + jnp.eye(C, dtype=jnp.float32)
    rhs = jnp.concatenate([v * beta[..., None], kb * jnp.exp(g)[..., None]], axis=-1)
    sol = lax.linalg.triangular_solve(a_mat, rhs, left_side=True, lower=True, unit_diagonal=True)
    u, w = sol[..., :dv], sol[..., dv:]
    attn = jnp.where(incl, jnp.einsum('bhncd,bhnsd->bhncs', q, k) * decay, 0.0)
    q_dec = q * jnp.exp(g)[..., None]
    k_dec = k * jnp.exp(g[..., -1:] - g)[..., None]
    g_tot = jnp.exp(g[..., -1])
    xs = tuple(jnp.moveaxis(t, 2, 0) for t in (u, w, q_dec, k_dec, attn, g_tot))

    def step(S, inp):
        u_c, w_c, q_c, k_c, attn_c, gt = inp
        v_new = u_c - jnp.einsum('bhcd,bhde->bhce', w_c, S)
        o = jnp.einsum('bhcd,bhde->bhce', q_c, S) + jnp.einsum('bhcs,bhse->bhce', attn_c, v_new)
        S = S * gt[..., None, None] + jnp.einsum('bhcd,bhce->bhde', k_c, v_new)
        return S, o

    S, o = lax.scan(step, s0.astype(jnp.float32), xs)
    o = jnp.moveaxis(o, 0, 2).reshape(B, H, n * C, dv)[:, :, :T]
    return jnp.moveaxis(o, 1, 2), S


def gdn_mixer(x, conv_buf, s0, w_in, conv_w, a_log, dt_bias, o_norm, w_out):
    B, T, _ = x.shape
    proj = x @ w_in
    qkv, z, b, a = jnp.split(proj, [GDN_CONV_DIM, GDN_CONV_DIM + GDN_V, GDN_CONV_DIM + GDN_V + GDN_HEADS], axis=-1)
    qkv_ext = jnp.concatenate([conv_buf.astype(qkv.dtype), qkv], axis=1)
    new_buf = qkv_ext[:, T:]
    qkv = jax.nn.silu(causal_dwconv(qkv_ext, conv_w))
    q, k, v = jnp.split(qkv, [GDN_QK, 2 * GDN_QK], axis=-1)
    shp = (B, T, GDN_HEADS, GDN_HEAD_DIM)
    beta = jax.nn.sigmoid(b.astype(jnp.float32))
    g = -jnp.exp(a_log.astype(jnp.float32)) * jax.nn.softplus(a.astype(jnp.float32) + dt_bias.astype(jnp.float32))
    o, s = chunk_gated_delta(q.reshape(shp), k.reshape(shp), v.reshape(shp), g, beta, s0)
    o = rms_norm(o, o_norm) * jax.nn.silu(z.reshape(shp).astype(jnp.float32))
    y = o.reshape(B, T, GDN_V).astype(x.dtype) @ w_out
    return y, new_buf, s.astype(s0.dtype)


def rwkv7_recurrence(r, decay, k, v, a, b, s0):
    xs = tuple(jnp.moveaxis(t.astype(jnp.float32), 1, 0) for t in (r, decay, k, v, a, b))

    def step(S, inp):
        r_t, w_t, k_t, v_t, a_t, b_t = inp
        sa = jnp.einsum('bhij,bhj->bhi', S, a_t)
        S = S * w_t[:, :, None, :] + sa[..., None] * b_t[:, :, None, :] + v_t[..., None] * k_t[:, :, None, :]
        return S, jnp.einsum('bhij,bhj->bhi', S, r_t)

    S, y = lax.scan(step, s0.astype(jnp.float32), xs)
    return jnp.moveaxis(y, 0, 1), S


def rwkv_mixer(x, shift_prev, s0, v_first, lerp, w_rkv, w0, w1, w2, a0, a1, a2, vres,
               g1, g2, k_k, k_a, r_k, ln_w, ln_b, w_o):
    B, T, D = x.shape
    x_prev = jnp.concatenate([shift_prev[:, None].astype(x.dtype), x[:, :-1]], axis=1)
    xs = x[:, :, None] + (x_prev - x)[:, :, None] * lerp
    rkv = jnp.einsum('btsd,sde->btse', xs[:, :, :3], w_rkv)
    r, k, v = rkv[:, :, 0], rkv[:, :, 1], rkv[:, :, 2]
    xv, xw, xa, xg = xs[:, :, 2], xs[:, :, 3], xs[:, :, 4], xs[:, :, 5]
    logw = -jax.nn.softplus(-(w0 + jnp.tanh(xw @ w1) @ w2).astype(jnp.float32)) - 0.5
    if v_first is None:
        v_first = v
    else:
        v0, v1, v2 = vres
        v = v + (v_first - v) * jax.nn.sigmoid(v0 + (xv @ v1) @ v2)
    a = jax.nn.sigmoid((a0 + (xa @ a1) @ a2).astype(jnp.float32))
    g = jax.nn.sigmoid(xg @ g1) @ g2

    def heads(t):
        return t.reshape(B, T, RWKV_HEADS, RWKV_HEAD)

    kk = l2_normalize(heads(k * k_k))
    k = k * (1.0 + (a - 1.0) * k_a)
    y, s = rwkv7_recurrence(heads(r), jnp.exp(-jnp.exp(heads(logw))), heads(k), heads(v), -kk, kk * heads(a), s0)
    mu = jnp.mean(y, axis=-1, keepdims=True)
    yc = y - mu
    y = yc * lax.rsqrt(jnp.mean(yc * yc, axis=-1, keepdims=True) + GN_EPS)
    y = y.reshape(B, T, D) * ln_w + ln_b
    bonus = jnp.sum(heads(r).astype(jnp.float32) * heads(k).astype(jnp.float32) * r_k.astype(jnp.float32),
                    axis=-1, keepdims=True) * heads(v).astype(jnp.float32)
    y = y + bonus.reshape(B, T, D)
    out = (y * g.astype(jnp.float32)).astype(x.dtype) @ w_o
    return out, x[:, -1], s.astype(s0.dtype), v_first


def trunk(h, gdn_s, gdn_buf, rw_s, rw_shift, p):
    out_gdn_s, out_gdn_buf, out_rw_s, out_rw_shift = [], [], [], []
    v_first = None
    for i in range(DEPTH):
        ng = p['norm_g'][i]
        h = h + 0.5 * rms_norm(swiglu(rms_norm(h, ng[0]), p['ffn_w_in'][i, 0], p['ffn_w_out'][i, 0]), ng[1])
        u = rms_norm(h, ng[2])
        j = i // N_MIXERS
        if i % N_MIXERS == 0:
            out, buf, s = gdn_mixer(u, gdn_buf[j], gdn_s[j], p['gdn_w_in'][j], p['gdn_conv_w'][j],
                                    p['gdn_a_log'][j], p['gdn_dt_bias'][j], p['gdn_o_norm'][j], p['gdn_w_out'][j])
            out_gdn_s.append(s)
            out_gdn_buf.append(buf)
        else:
            vres = None if j == 0 else (p['rwkv_v0'][j - 1], p['rwkv_v1'][j - 1], p['rwkv_v2'][j - 1])
            out, shift, s, v_first = rwkv_mixer(
                u, rw_shift[j], rw_s[j], v_first, p['rwkv_lerp'][j], p['rwkv_w_rkv'][j],
                p['rwkv_w0'][j], p['rwkv_w1'][j], p['rwkv_w2'][j], p['rwkv_a0'][j], p['rwkv_a1'][j], p['rwkv_a2'][j],
                vres, p['rwkv_g1'][j], p['rwkv_g2'][j], p['rwkv_k_k'][j], p['rwkv_k_a'][j], p['rwkv_r_k'][j],
                p['rwkv_ln_w'][j], p['rwkv_ln_b'][j], p['rwkv_w_o'][j])
            out_rw_s.append(s)
            out_rw_shift.append(shift)
        h = h + rms_norm(out, ng[3])
        h = h + 0.5 * rms_norm(swiglu(rms_norm(h, ng[4]), p['ffn_w_in'][i, 1], p['ffn_w_out'][i, 1]), ng[5])
    return h, jnp.stack(out_gdn_s), jnp.stack(out_gdn_buf), jnp.stack(out_rw_s), jnp.stack(out_rw_shift)


def setup_inputs(seed: int = 0) -> dict:
    key = jax.random.key(seed)
    ks = iter(jax.random.split(key, 48))

    def nrm(shape, scale, shift=0.0):
        return shift + scale * jax.random.normal(next(ks), shape, jnp.float32)

    D = D_MODEL
    nv = N_RWKV - 1
    return {
        'x_prompt': nrm((BATCH, SEQ, D), 1.0),
        'x_sample': nrm((DEC_BATCH, DEC_SEQ, D), 1.0),
        'state_gdn': nrm((N_GDN, DEC_BATCH, GDN_HEADS, GDN_HEAD_DIM, GDN_HEAD_DIM), 0.3),
        'state_gdn_conv': nrm((N_GDN, DEC_BATCH, GDN_CONV - 1, GDN_CONV_DIM), 1.0),
        'state_rwkv': nrm((N_RWKV, DEC_BATCH, RWKV_HEADS, RWKV_HEAD, RWKV_HEAD), 0.3),
        'state_rwkv_shift': nrm((N_RWKV, DEC_BATCH, D), 1.0),
        'norm_g': nrm((DEPTH, N_NORMS, D), 0.05, 1.0),
        'ffn_w_in': nrm((DEPTH, 2, D, 2 * D_FF), D ** -0.5),
        'ffn_w_out': nrm((DEPTH, 2, D_FF, D), D_FF ** -0.5),
        'gdn_w_in': nrm((N_GDN, D, GDN_IN), D ** -0.5),
        'gdn_conv_w': nrm((N_GDN, GDN_CONV, GDN_CONV_DIM), 0.5),
        'gdn_a_log': jnp.log(jax.random.uniform(next(ks), (N_GDN, GDN_HEADS), jnp.float32, 1.0, 16.0)),
        'gdn_dt_bias': nrm((N_GDN, GDN_HEADS), 0.1),
        'gdn_o_norm': nrm((N_GDN, GDN_HEAD_DIM), 0.05, 1.0),
        'gdn_w_out': nrm((N_GDN, GDN_V, D), GDN_V ** -0.5),
        'rwkv_lerp': jax.random.uniform(next(ks), (N_RWKV, 6, D), jnp.float32),
        'rwkv_w_rkv': nrm((N_RWKV, 3, D, D), D ** -0.5),
        'rwkv_w0': nrm((N_RWKV, D), 0.5, -1.5),
        'rwkv_w1': nrm((N_RWKV, D, D_DECAY_LORA), D ** -0.5),
        'rwkv_w2': nrm((N_RWKV, D_DECAY_LORA, D), 0.5 * D_DECAY_LORA ** -0.5),
        'rwkv_a0': nrm((N_RWKV, D), 0.1),
        'rwkv_a1': nrm((N_RWKV, D, D_AAA_LORA), D ** -0.5),
        'rwkv_a2': nrm((N_RWKV, D_AAA_LORA, D), 0.5 * D_AAA_LORA ** -0.5),
        'rwkv_v0': nrm((nv, D), 0.1),
        'rwkv_v1': nrm((nv, D, D_MV_LORA), D ** -0.5),
        'rwkv_v2': nrm((nv, D_MV_LORA, D), 0.5 * D_MV_LORA ** -0.5),
        'rwkv_g1': nrm((N_RWKV, D, D_GATE_LORA), D ** -0.5),
        'rwkv_g2': nrm((N_RWKV, D_GATE_LORA, D), D_GATE_LORA ** -0.5),
        'rwkv_k_k': nrm((N_RWKV, D), 0.05, 0.85),
        'rwkv_k_a': nrm((N_RWKV, D), 0.05, 1.0),
        'rwkv_r_k': nrm((N_RWKV, RWKV_HEADS, RWKV_HEAD), 0.1),
        'rwkv_ln_w': nrm((N_RWKV, D), 0.05, 1.0),
        'rwkv_ln_b': nrm((N_RWKV, D), 0.01),
        'rwkv_w_o': nrm((N_RWKV, D, D), D ** -0.5),
    }


def reference(x_prompt, x_sample, state_gdn, state_gdn_conv, state_rwkv, state_rwkv_shift,
              norm_g, ffn_w_in, ffn_w_out, gdn_w_in, gdn_conv_w, gdn_a_log, gdn_dt_bias, gdn_o_norm, gdn_w_out,
              rwkv_lerp, rwkv_w_rkv, rwkv_w0, rwkv_w1, rwkv_w2, rwkv_a0, rwkv_a1, rwkv_a2,
              rwkv_v0, rwkv_v1, rwkv_v2, rwkv_g1, rwkv_g2, rwkv_k_k, rwkv_k_a, rwkv_r_k,
              rwkv_ln_w, rwkv_ln_b, rwkv_w_o):
    p = dict(norm_g=norm_g, ffn_w_in=ffn_w_in, ffn_w_out=ffn_w_out,
             gdn_w_in=gdn_w_in, gdn_conv_w=gdn_conv_w, gdn_a_log=gdn_a_log, gdn_dt_bias=gdn_dt_bias,
             gdn_o_norm=gdn_o_norm, gdn_w_out=gdn_w_out,
             rwkv_lerp=rwkv_lerp, rwkv_w_rkv=rwkv_w_rkv, rwkv_w0=rwkv_w0, rwkv_w1=rwkv_w1, rwkv_w2=rwkv_w2,
             rwkv_a0=rwkv_a0, rwkv_a1=rwkv_a1, rwkv_a2=rwkv_a2, rwkv_v0=rwkv_v0, rwkv_v1=rwkv_v1, rwkv_v2=rwkv_v2,
             rwkv_g1=rwkv_g1, rwkv_g2=rwkv_g2, rwkv_k_k=rwkv_k_k, rwkv_k_a=rwkv_k_a, rwkv_r_k=rwkv_r_k,
             rwkv_ln_w=rwkv_ln_w, rwkv_ln_b=rwkv_ln_b, rwkv_w_o=rwkv_w_o)
    dt = x_prompt.dtype
    zero_gdn = jnp.zeros((N_GDN, BATCH, GDN_HEADS, GDN_HEAD_DIM, GDN_HEAD_DIM), dt)
    zero_conv = jnp.zeros((N_GDN, BATCH, GDN_CONV - 1, GDN_CONV_DIM), dt)
    zero_rwkv = jnp.zeros((N_RWKV, BATCH, RWKV_HEADS, RWKV_HEAD, RWKV_HEAD), dt)
    zero_shift = jnp.zeros((N_RWKV, BATCH, D_MODEL), dt)
    y_prompt, gdn_s_p, gdn_c_p, rw_s_p, rw_sh_p = trunk(x_prompt, zero_gdn, zero_conv, zero_rwkv, zero_shift, p)
    y_sample, gdn_s_s, gdn_c_s, rw_s_s, rw_sh_s = trunk(x_sample, state_gdn, state_gdn_conv, state_rwkv, state_rwkv_shift, p)
    return (y_prompt, y_sample, gdn_s_p, gdn_c_p, rw_s_p, rw_sh_p, gdn_s_s, gdn_c_s, rw_s_s, rw_sh_s)
```

```python
import functools

import jax
import jax.numpy as jnp
from jax import lax
from jax.experimental import pallas as pl
from jax.experimental.pallas import tpu as pltpu

F32 = jnp.float32
BF16 = jnp.bfloat16

NORM_EPS = 1e-6
L2_EPS = 1e-6
GN_EPS = 64e-5

GDN_HEADS = 8
GDN_HEAD_DIM = 128
GDN_QKV = 3 * GDN_HEADS * GDN_HEAD_DIM
GDN_CONV = 4
RWKV_HEAD = 64

V7X_VMEM_BYTES = 64 * 1024 * 1024
VMEM_LIMIT_BYTES = V7X_VMEM_BYTES * 3 // 4
SUBLANES = 8
LANES = 128


def _params(*semantics):
    return pltpu.CompilerParams(dimension_semantics=semantics, vmem_limit_bytes=VMEM_LIMIT_BYTES)


def _rms(x, g):
    return x * lax.rsqrt(jnp.mean(x * x, axis=-1, keepdims=True) + NORM_EPS) * g


def _softplus(x):
    return jnp.maximum(x, 0.0) + jnp.log1p(jnp.exp(-jnp.abs(x)))


def _dot(a, b):
    return jnp.dot(a.astype(BF16), b.astype(BF16), preferred_element_type=F32)


def _dot_nt(a, b):
    return lax.dot_general(a.astype(BF16), b.astype(BF16), (((1,), (1,)), ((), ())),
                           preferred_element_type=F32)


def _dot_tn(a, b):
    return lax.dot_general(a.astype(BF16), b.astype(BF16), (((0,), (0,)), ((), ())),
                           preferred_element_type=F32)


def _split2(a):
    hi = a.astype(BF16)
    lo = (a - hi.astype(F32)).astype(BF16)
    return hi, lo


def _dot_hi(a, b):
    ah, al = _split2(a)
    bh, bl = _split2(b)
    return (jnp.dot(ah, bh, preferred_element_type=F32)
            + (jnp.dot(ah, bl, preferred_element_type=F32) + jnp.dot(al, bh, preferred_element_type=F32)))


def _split3(a):
    hi = a.astype(BF16)
    r = a - hi.astype(F32)
    mid = r.astype(BF16)
    lo = (r - mid.astype(F32)).astype(BF16)
    return hi, mid, lo


def _tri_incl(c):
    ri = lax.broadcasted_iota(jnp.int32, (c, c), 0)
    ci = lax.broadcasted_iota(jnp.int32, (c, c), 1)
    return ri >= ci, ri > ci


def _cumsum_rows(tri_bf, x):
    hi, mid, lo = _split3(x)
    return (jnp.dot(tri_bf, hi, preferred_element_type=F32)
            + (jnp.dot(tri_bf, mid, preferred_element_type=F32) + jnp.dot(tri_bf, lo, preferred_element_type=F32)))


def _cumsum_cols(tri_bf, x):
    hi, mid, lo = _split3(x)
    dn = (((1,), (1,)), ((), ()))
    return (lax.dot_general(hi, tri_bf, dn, preferred_element_type=F32)
            + (lax.dot_general(mid, tri_bf, dn, preferred_element_type=F32)
               + lax.dot_general(lo, tri_bf, dn, preferred_element_type=F32)))


def _unit_lower_inverse(a, c):
    ri = lax.broadcasted_iota(jnp.int32, (c, c), 0)
    ci = lax.broadcasted_iota(jnp.int32, (c, c), 1)
    eye = (ri == ci).astype(F32)
    n = jnp.where((ri >> 3) == (ci >> 3), -a, 0.0)
    p = eye + n
    n2 = _dot_hi(n, n)
    p = p + _dot_hi(p, n2)
    n4 = _dot_hi(n2, n2)
    p = p + _dot_hi(p, n4)
    s = SUBLANES
    while s < c:
        e = _dot_hi(_dot_hi(p, a), p)
        shift = (2 * s).bit_length() - 1
        merge = ((ri >> shift) == (ci >> shift)) & ((ri & s) != 0) & ((ci & s) == 0)
        p = p - jnp.where(merge, e, 0.0)
        s *= 2
    return p


def _ffn_body(h_ref, gin_ref, gout_ref, gnext_ref, wg_ref, wu_ref, wo_ref, *rest, emit_u):
    if emit_u:
        hout_ref, u_ref, xn_sc, acc_sc = rest
    else:
        hout_ref, xn_sc, acc_sc = rest
    j = pl.program_id(1)

    @pl.when(j == 0)
    def _():
        xn_sc[...] = _rms(h_ref[...], gin_ref[...]).astype(BF16)
        acc_sc[...] = jnp.zeros_like(acc_sc)

    xn = xn_sc[...]
    gate = jnp.dot(xn, wg_ref[...], preferred_element_type=F32)
    up = jnp.dot(xn, wu_ref[...], preferred_element_type=F32)
    act = (gate * jax.nn.sigmoid(gate)) * up
    acc_sc[...] += jnp.dot(act.astype(BF16), wo_ref[...], preferred_element_type=F32)

    @pl.when(j == pl.num_programs(1) - 1)
    def _():
        hn = h_ref[...] + 0.5 * _rms(acc_sc[...], gout_ref[...])
        hout_ref[...] = hn
        if emit_u:
            u_ref[...] = _rms(hn, gnext_ref[...])


def _ffn(h, g_in, g_out, g_next, w_in, w_out, *, emit_u, tm, tf):
    n, d = h.shape
    f = w_out.shape[0]
    nf = f // tf
    row = pl.BlockSpec((tm, d), lambda i, j: (i, 0))
    vec = pl.BlockSpec((1, d), lambda i, j: (0, 0))
    out_shape = [jax.ShapeDtypeStruct((n, d), F32)]
    out_specs = [row]
    if emit_u:
        out_shape.append(jax.ShapeDtypeStruct((n, d), F32))
        out_specs.append(row)
    return pl.pallas_call(
        functools.partial(_ffn_body, emit_u=emit_u),
        grid=(n // tm, nf),
        in_specs=[row, vec, vec, vec,
                  pl.BlockSpec((d, tf), lambda i, j: (0, j)),
                  pl.BlockSpec((d, tf), lambda i, j: (0, j + nf)),
                  pl.BlockSpec((tf, d), lambda i, j: (j, 0))],
        out_specs=out_specs,
        out_shape=out_shape,
        scratch_shapes=[pltpu.VMEM((tm, d), BF16), pltpu.VMEM((tm, d), F32)],
        compiler_params=_params("parallel", "arbitrary"),
        name="ffn_u" if emit_u else "ffn",
    )(h, g_in, g_out, g_next, w_in, w_in, w_out)


def _gdn_proj_body(u_ref, w_ref, wba_ref, main_ref, ba_ref, xb_sc):
    j = pl.program_id(1)

    @pl.when(j == 0)
    def _():
        xb = u_ref[...].astype(BF16)
        xb_sc[...] = xb
        ba_ref[...] = jnp.dot(xb, wba_ref[...], preferred_element_type=F32)

    main_ref[...] = jnp.dot(xb_sc[...], w_ref[...], preferred_element_type=F32)


def _gdn_proj(u, w_main, w_ba, *, tm, tn):
    n, d = u.shape
    nm = w_main.shape[1]
    return pl.pallas_call(
        _gdn_proj_body,
        grid=(n // tm, nm // tn),
        in_specs=[pl.BlockSpec((tm, d), lambda i, j: (i, 0)),
                  pl.BlockSpec((d, tn), lambda i, j: (0, j)),
                  pl.BlockSpec((d, LANES), lambda i, j: (0, 0))],
        out_specs=[pl.BlockSpec((tm, tn), lambda i, j: (i, j)),
                   pl.BlockSpec((tm, LANES), lambda i, j: (i, 0))],
        out_shape=[jax.ShapeDtypeStruct((n, nm), F32), jax.ShapeDtypeStruct((n, LANES), F32)],
        scratch_shapes=[pltpu.VMEM((tm, d), BF16)],
        compiler_params=_params("parallel", "arbitrary"),
        name="gdn_proj",
    )(u, w_main, w_ba)


def _gdn_rec_body(cur_ref, prev_ref, init_ref, ba_ref, bat_ref, cw_ref, alr_ref, dbr_ref, alc_ref, dbc_ref,
                  s0_ref, o_ref, sout_ref, ext_sc, s_sc, *, chunk, t_valid):
    c = pl.program_id(1)
    hd = GDN_HEAD_DIM
    nh = GDN_HEADS

    @pl.when(c == 0)
    def _():
        s_sc[...] = s0_ref[0]
        ext_sc[0:SUBLANES, :] = init_ref[0]

    @pl.when(c > 0)
    def _():
        ext_sc[0:SUBLANES, :] = prev_ref[0]

    ext_sc[SUBLANES:SUBLANES + chunk, :] = cur_ref[0]

    first = SUBLANES - (GDN_CONV - 1)
    conv = cw_ref[0:1, :] * ext_sc[first:first + chunk, :]
    for tap in range(1, GDN_CONV):
        conv = conv + cw_ref[tap:tap + 1, :] * ext_sc[first + tap:first + tap + chunk, :]
    qkv = conv * jax.nn.sigmoid(conv)

    ba = ba_ref[0]
    bat = bat_ref[0, 0]
    g_cols = -jnp.exp(alr_ref[...]) * _softplus(ba + dbr_ref[...])
    g_rows = -jnp.exp(alc_ref[...]) * _softplus(bat + dbc_ref[...])
    beta_cols = jax.nn.sigmoid(ba)
    if t_valid < chunk:
        row_ok = lax.broadcasted_iota(jnp.int32, (chunk, 1), 0) < t_valid
        col_ok = lax.broadcasted_iota(jnp.int32, (1, chunk), 1) < t_valid
        qkv = jnp.where(row_ok, qkv, 0.0)
        g_cols = jnp.where(row_ok, g_cols, 0.0)
        g_rows = jnp.where(col_ok, g_rows, 0.0)

    incl, strict = _tri_incl(chunk)
    tri_bf = incl.astype(BF16)
    gc_cols = _cumsum_rows(tri_bf, g_cols)
    gc_rows = _cumsum_cols(tri_bf, g_rows)

    for h in range(nh):
        q = qkv[:, h * hd:(h + 1) * hd]
        k = qkv[:, (nh + h) * hd:(nh + h + 1) * hd]
        v = qkv[:, (2 * nh + h) * hd:(2 * nh + h + 1) * hd]
        qn = q * lax.rsqrt(jnp.sum(q * q, axis=-1, keepdims=True) + L2_EPS) * (hd ** -0.5)
        kn = k * lax.rsqrt(jnp.sum(k * k, axis=-1, keepdims=True) + L2_EPS)
        beta = beta_cols[:, h:h + 1]
        gc = gc_cols[:, nh + h:nh + h + 1]
        gr = gc_rows[nh + h:nh + h + 1, :]
        decay = jnp.exp(jnp.where(incl, gc - gr, -jnp.inf))
        kb = kn * beta
        a_mat = jnp.where(strict, _dot_nt(kb, kn) * decay, 0.0)
        t_mat = _unit_lower_inverse(a_mat, chunk)
        eg = jnp.exp(gc)
        u = _dot(t_mat, v * beta)
        w = _dot(t_mat, kb * eg)
        attn = jnp.where(incl, _dot_nt(qn, kn) * decay, 0.0)
        s = s_sc[h]
        v_new = u - _dot(w, s)
        o = _dot(qn * eg, s) + _dot(attn, v_new)
        g_last = gc[chunk - 1:chunk, :]
        k_dec = kn * jnp.exp(g_last - gc)
        s_sc[h] = s * jnp.exp(g_last) + _dot_tn(k_dec, v_new)
        o_ref[0, :, h * hd:(h + 1) * hd] = o

    @pl.when(c == pl.num_programs(1) - 1)
    def _():
        sout_ref[0] = s_sc[...]


def _gdn_rec(qkv, conv_init, ba, bat, conv_w, alr, dbr, alc, dbc, s0, *, chunk, t_valid):
    bsz, tp, width = qkv.shape
    nc = tp // chunk
    cb = chunk // SUBLANES
    dv = GDN_HEADS * GDN_HEAD_DIM
    full2 = lambda shape: pl.BlockSpec(shape, lambda b, c: (0, 0))
    return pl.pallas_call(
        functools.partial(_gdn_rec_body, chunk=chunk, t_valid=t_valid),
        grid=(bsz, nc),
        in_specs=[pl.BlockSpec((1, chunk, GDN_QKV), lambda b, c: (b, c, 0)),
                  pl.BlockSpec((1, SUBLANES, GDN_QKV), lambda b, c: (b, jnp.maximum(c * cb - 1, 0), 0)),
                  pl.BlockSpec((1, SUBLANES, GDN_QKV), lambda b, c: (b, 0, 0)),
                  pl.BlockSpec((1, chunk, LANES), lambda b, c: (b, c, 0)),
                  pl.BlockSpec((1, 1, 2 * GDN_HEADS, chunk), lambda b, c: (b, c, 0, 0)),
                  full2(conv_w.shape), full2(alr.shape), full2(dbr.shape), full2(alc.shape), full2(dbc.shape),
                  pl.BlockSpec((1, GDN_HEADS, GDN_HEAD_DIM, GDN_HEAD_DIM), lambda b, c: (b, 0, 0, 0))],
        out_specs=[pl.BlockSpec((1, chunk, dv), lambda b, c: (b, c, 0)),
                   pl.BlockSpec((1, GDN_HEADS, GDN_HEAD_DIM, GDN_HEAD_DIM), lambda b, c: (b, 0, 0, 0))],
        out_shape=[jax.ShapeDtypeStruct((bsz, tp, dv), F32),
                   jax.ShapeDtypeStruct((bsz, GDN_HEADS, GDN_HEAD_DIM, GDN_HEAD_DIM), F32)],
        scratch_shapes=[pltpu.VMEM((chunk + SUBLANES, GDN_QKV), F32),
                        pltpu.VMEM((GDN_HEADS, GDN_HEAD_DIM, GDN_HEAD_DIM), F32)],
        compiler_params=_params("parallel", "arbitrary"),
        name="gdn_rec",
    )(qkv, qkv, conv_init, ba, bat, conv_w, alr, dbr, alc, dbc, s0)


def _gdn_post_body(o_ref, z_ref, h_ref, on_ref, w_ref, g_ref, hout_ref, gated_sc):
    hd = GDN_HEAD_DIM
    for h in range(GDN_HEADS):
        sl = slice(h * hd, (h + 1) * hd)
        o = o_ref[:, sl]
        z = z_ref[:, sl]
        on = o * lax.rsqrt(jnp.mean(o * o, axis=-1, keepdims=True) + NORM_EPS) * on_ref[...]
        gated_sc[:, sl] = (on * (z * jax.nn.sigmoid(z))).astype(BF16)
    y = jnp.dot(gated_sc[...], w_ref[...], preferred_element_type=F32)
    hout_ref[...] = h_ref[...] + _rms(y, g_ref[...])


def _gdn_post(o, main, h, o_norm, w_out, g, *, tm):
    n, d = h.shape
    dv = o.shape[1]
    zblk = GDN_QKV // dv
    row = pl.BlockSpec((tm, d), lambda i: (i, 0))
    return pl.pallas_call(
        _gdn_post_body,
        grid=(n // tm,),
        in_specs=[pl.BlockSpec((tm, dv), lambda i: (i, 0)),
                  pl.BlockSpec((tm, dv), lambda i: (i, zblk)),
                  row,
                  pl.BlockSpec((1, GDN_HEAD_DIM), lambda i: (0, 0)),
                  pl.BlockSpec((dv, d), lambda i: (0, 0)),
                  pl.BlockSpec((1, d), lambda i: (0, 0))],
        out_specs=row,
        out_shape=jax.ShapeDtypeStruct((n, d), F32),
        scratch_shapes=[pltpu.VMEM((tm, dv), BF16)],
        compiler_params=_params("parallel"),
        name="gdn_post",
    )(o, main, h, o_norm, w_out, g)


def _rwkv_proj_body(*refs, has_vres):
    if has_vres:
        (x_ref, xp_ref, vf_ref, lerp_ref, wr_ref, wk_ref, wv_ref, w0_ref, w1_ref, w2_ref, a0_ref, a1_ref,
         a2_ref, g1_ref, g2_ref, v0_ref, v1_ref, v2_ref, r_ref, k_ref, v_ref, lw_ref, a_ref, g_ref) = refs
    else:
        (x_ref, xp_ref, lerp_ref, wr_ref, wk_ref, wv_ref, w0_ref, w1_ref, w2_ref, a0_ref, a1_ref,
         a2_ref, g1_ref, g2_ref, r_ref, k_ref, v_ref, lw_ref, a_ref, g_ref) = refs
    x = x_ref[...]
    dx = xp_ref[...] - x

    def mix(s):
        return (x + dx * lerp_ref[s:s + 1, :]).astype(BF16)

    r_ref[...] = jnp.dot(mix(0), wr_ref[...], preferred_element_type=F32)
    k_ref[...] = jnp.dot(mix(1), wk_ref[...], preferred_element_type=F32)
    xv = mix(2)
    v = jnp.dot(xv, wv_ref[...], preferred_element_type=F32)
    if has_vres:
        gate = jax.nn.sigmoid(v0_ref[...] + _dot(jnp.dot(xv, v1_ref[...], preferred_element_type=F32), v2_ref[...]))
        v = v + (vf_ref[...] - v) * gate
    v_ref[...] = v
    dec = w0_ref[...] + _dot(jnp.tanh(jnp.dot(mix(3), w1_ref[...], preferred_element_type=F32)), w2_ref[...])
    logw = -_softplus(-dec) - 0.5
    lw_ref[...] = -jnp.exp(logw)
    a_ref[...] = jax.nn.sigmoid(a0_ref[...] + _dot(jnp.dot(mix(4), a1_ref[...], preferred_element_type=F32),
                                                   a2_ref[...]))
    g_ref[...] = _dot(jax.nn.sigmoid(jnp.dot(mix(5), g1_ref[...], preferred_element_type=F32)), g2_ref[...])


def _rwkv_proj(x, xp, v_first, lerp, wr, wk, wv, w0, w1, w2, a0, a1, a2, g1, g2, vres, *, tm):
    n, d = x.shape
    has_vres = vres is not None
    row = pl.BlockSpec((tm, d), lambda i: (i, 0))
    full = lambda a: pl.BlockSpec(a.shape, lambda i: (0, 0))
    args = [x, xp] + ([v_first] if has_vres else []) + [lerp, wr, wk, wv, w0, w1, w2, a0, a1, a2, g1, g2]
    in_specs = [row, row] + ([row] if has_vres else []) + [full(a) for a in args[(3 if has_vres else 2):]]
    if has_vres:
        args += list(vres)
        in_specs += [full(a) for a in vres]
    return pl.pallas_call(
        functools.partial(_rwkv_proj_body, has_vres=has_vres),
        grid=(n // tm,),
        in_specs=in_specs,
        out_specs=[row] * 6,
        out_shape=[jax.ShapeDtypeStruct((n, d), F32)] * 6,
        compiler_params=_params("parallel"),
        name="rwkv_proj_vres" if has_vres else "rwkv_proj",
    )(*args)


def _rwkv_rec_body(r_ref, k_ref, v_ref, lw_ref, a_ref, kk_ref, ka_ref, rk_ref, lnw_ref, lnb_ref, s0_ref,
                   y_ref, sout_ref, s_sc, *, chunk, t_valid, heads):
    c = pl.program_id(1)
    hd = RWKV_HEAD

    @pl.when(c == 0)
    def _():
        s_sc[...] = s0_ref[0]

    r_all = r_ref[0]
    k_all = k_ref[0]
    v_all = v_ref[0]
    lw_all = lw_ref[0]
    a_all = a_ref[0]
    if t_valid < chunk:
        row_ok = lax.broadcasted_iota(jnp.int32, (chunk, 1), 0) < t_valid
        r_all = jnp.where(row_ok, r_all, 0.0)
        k_all = jnp.where(row_ok, k_all, 0.0)
        v_all = jnp.where(row_ok, v_all, 0.0)
        lw_all = jnp.where(row_ok, lw_all, 0.0)

    incl, strict = _tri_incl(chunk)
    gc_all = _cumsum_rows(incl.astype(BF16), lw_all)
    kkp = k_all * kk_ref[...]
    kmod_all = k_all * (1.0 + (a_all - 1.0) * ka_ref[...])

    for h in range(heads):
        sl = slice(h * hd, (h + 1) * hd)
        r = r_all[:, sl]
        v = v_all[:, sl]
        ag = a_all[:, sl]
        gc = gc_all[:, sl]
        gcp = gc - lw_all[:, sl]
        kmod = kmod_all[:, sl]
        kx = kkp[:, sl]
        kk = kx * lax.rsqrt(jnp.sum(kx * kx, axis=-1, keepdims=True) + L2_EPS)
        bv = kk * ag
        e_neg = jnp.exp(-gc)
        at = -kk * jnp.exp(gcp)
        rt = r * jnp.exp(gc)
        bt = bv * e_neg
        kt = kmod * e_neg
        a_ab = jnp.where(strict, _dot_nt(at, bt), 0.0)
        a_ak = jnp.where(strict, _dot_nt(at, kt), 0.0)
        a_rb = jnp.where(incl, _dot_nt(rt, bt), 0.0)
        a_rk = jnp.where(incl, _dot_nt(rt, kt), 0.0)
        t_mat = _unit_lower_inverse(-a_ab, chunk)
        s = s_sc[h]
        u = _dot(t_mat, _dot_nt(at, s) + _dot(a_ak, v))
        y = _dot_nt(rt, s) + _dot(a_rb, u) + _dot(a_rk, v)
        g_last = gc[chunk - 1:chunk, :]
        e_rem = jnp.exp(g_last - gc)
        s_sc[h] = s * jnp.exp(g_last) + _dot_tn(u, bv * e_rem) + _dot_tn(v, kmod * e_rem)

        mu = jnp.mean(y, axis=-1, keepdims=True)
        yc = y - mu
        yn = yc * lax.rsqrt(jnp.mean(yc * yc, axis=-1, keepdims=True) + GN_EPS)
        bonus = jnp.sum(r * kmod * rk_ref[:, sl], axis=-1, keepdims=True) * v
        y_ref[0, :, sl] = yn * lnw_ref[:, sl] + lnb_ref[:, sl] + bonus

    @pl.when(c == pl.num_programs(1) - 1)
    def _():
        sout_ref[0] = s_sc[...]


def _rwkv_rec(r, k, v, lw, a, k_k, k_a, r_k, ln_w, ln_b, s0, *, chunk, t_valid):
    bsz, tp, d = r.shape
    heads = d // RWKV_HEAD
    nc = tp // chunk
    seq = pl.BlockSpec((1, chunk, d), lambda b, c: (b, c, 0))
    vec = pl.BlockSpec((1, d), lambda b, c: (0, 0))
    st = pl.BlockSpec((1, heads, RWKV_HEAD, RWKV_HEAD), lambda b, c: (b, 0, 0, 0))
    return pl.pallas_call(
        functools.partial(_rwkv_rec_body, chunk=chunk, t_valid=t_valid, heads=heads),
        grid=(bsz, nc),
        in_specs=[seq] * 5 + [vec] * 5 + [st],
        out_specs=[seq, st],
        out_shape=[jax.ShapeDtypeStruct((bsz, tp, d), F32),
                   jax.ShapeDtypeStruct((bsz, heads, RWKV_HEAD, RWKV_HEAD), F32)],
        scratch_shapes=[pltpu.VMEM((heads, RWKV_HEAD, RWKV_HEAD), F32)],
        compiler_params=_params("parallel", "arbitrary"),
        name="rwkv_rec",
    )(r, k, v, lw, a, k_k, k_a, r_k, ln_w, ln_b, s0)


def _rwkv_post_body(y_ref, gate_ref, h_ref, w_ref, g_ref, hout_ref):
    out = jnp.dot((y_ref[...] * gate_ref[...]).astype(BF16), w_ref[...], preferred_element_type=F32)
    hout_ref[...] = h_ref[...] + _rms(out, g_ref[...])


def _rwkv_post(y, gate, h, w_o, g, *, tm):
    n, d = h.shape
    row = pl.BlockSpec((tm, d), lambda i: (i, 0))
    return pl.pallas_call(
        _rwkv_post_body,
        grid=(n // tm,),
        in_specs=[row, row, row, pl.BlockSpec((d, d), lambda i: (0, 0)), pl.BlockSpec((1, d), lambda i: (0, 0))],
        out_specs=row,
        out_shape=jax.ShapeDtypeStruct((n, d), F32),
        compiler_params=_params("parallel"),
        name="rwkv_post",
    )(y, gate, h, w_o, g)


def _pad_time(x, tp):
    t = x.shape[1]
    if t == tp:
        return x
    return jnp.pad(x, [(0, 0), (0, tp - t)] + [(0, 0)] * (x.ndim - 2))


def _tiles(n):
    return 1024 if n % 1024 == 0 else n


def _trunk(h, bsz, t, gdn_s, gdn_buf, rw_s, rw_shift, p, chunk):
    n, d = h.shape
    depth = p["norm_g"].shape[0]
    tm = _tiles(n)
    tp = -(-t // chunk) * chunk
    out_gdn_s, out_gdn_buf, out_rw_s, out_rw_shift = [], [], [], []
    v_first = None
    for i in range(depth):
        ng = p["norm_g"][i]
        row = lambda s: ng[s][None, :]
        h, u = _ffn(h, row(0), row(1), row(2), p["ffn_w_in"][i][0], p["ffn_w_out"][i][0],
                    emit_u=True, tm=tm, tf=256)
        j = i // 2
        if i % 2 == 0:
            main, ba = _gdn_proj(u, p["gdn_w_main"][j], p["gdn_w_ba"][j], tm=tm, tn=1024)
            main3 = main.reshape(bsz, t, main.shape[1])
            out_gdn_buf.append(main3[:, t - (GDN_CONV - 1):, :GDN_QKV])
            ba3 = _pad_time(ba.reshape(bsz, t, LANES), tp)
            bat = ba3[:, :, :2 * GDN_HEADS].reshape(bsz, tp // chunk, chunk, 2 * GDN_HEADS).transpose(0, 1, 3, 2)
            conv_init = jnp.pad(gdn_buf[j], [(0, 0), (SUBLANES - (GDN_CONV - 1), 0), (0, 0)])
            qkv = main3 if tp == t else _pad_time(main3[:, :, :GDN_QKV], tp)
            o, s = _gdn_rec(qkv, conv_init, ba3, bat, p["gdn_conv_w"][j], p["gdn_alr"][j], p["gdn_dbr"][j],
                            p["gdn_alc"][j], p["gdn_dbc"][j], gdn_s[j], chunk=chunk, t_valid=min(t, chunk))
            out_gdn_s.append(s)
            o2 = o[:, :t].reshape(n, o.shape[2])
            h = _gdn_post(o2, main, h, p["gdn_o_norm"][j], p["gdn_w_out"][j], row(3), tm=min(tm, 512))
        else:
            u3 = u.reshape(bsz, t, d)
            out_rw_shift.append(u3[:, -1])
            xp = jnp.concatenate([rw_shift[j][:, None], u3[:, :-1]], axis=1).reshape(n, d)
            vres = None if j == 0 else (p["rwkv_v0"][j - 1], p["rwkv_v1"][j - 1], p["rwkv_v2"][j - 1])
            r, k, v, lw, a, g = _rwkv_proj(
                u, xp, v_first, p["rwkv_lerp"][j], p["rwkv_w_r"][j], p["rwkv_w_k"][j], p["rwkv_w_v"][j],
                p["rwkv_w0"][j], p["rwkv_w1"][j], p["rwkv_w2"][j], p["rwkv_a0"][j], p["rwkv_a1"][j],
                p["rwkv_a2"][j], p["rwkv_g1"][j], p["rwkv_g2"][j], vres, tm=min(tm, 256))
            if v_first is None:
                v_first = v
            seq = lambda x: _pad_time(x.reshape(bsz, t, d), tp)
            y, s = _rwkv_rec(seq(r), seq(k), seq(v), seq(lw), seq(a), p["rwkv_k_k"][j], p["rwkv_k_a"][j],
                             p["rwkv_r_k"][j], p["rwkv_ln_w"][j], p["rwkv_ln_b"][j], rw_s[j],
                             chunk=chunk, t_valid=min(t, chunk))
            out_rw_s.append(s)
            h = _rwkv_post(y[:, :t].reshape(n, d), g, h, p["rwkv_w_o"][j], row(3), tm=min(tm, 512))
        h = _ffn(h, row(4), row(5), row(5), p["ffn_w_in"][i][1], p["ffn_w_out"][i][1],
                 emit_u=False, tm=tm, tf=256)[0]
    return (h.reshape(bsz, t, d), jnp.stack(out_gdn_s), jnp.stack(out_gdn_buf), jnp.stack(out_rw_s),
            jnp.stack(out_rw_shift))


def kernel(x_prompt, x_sample, state_gdn, state_gdn_conv, state_rwkv, state_rwkv_shift, norm_g, ffn_w_in, ffn_w_out, gdn_w_in, gdn_conv_w, gdn_a_log, gdn_dt_bias, gdn_o_norm, gdn_w_out, rwkv_lerp, rwkv_w_rkv, rwkv_w0, rwkv_w1, rwkv_w2, rwkv_a0, rwkv_a1, rwkv_a2, rwkv_v0, rwkv_v1, rwkv_v2, rwkv_g1, rwkv_g2, rwkv_k_k, rwkv_k_a, rwkv_r_k, rwkv_ln_w, rwkv_ln_b, rwkv_w_o):
    bsz, t, d = x_prompt.shape
    dbsz, dt, _ = x_sample.shape
    n_gdn = gdn_w_in.shape[0]
    n_rwkv = rwkv_w_rkv.shape[0]
    nh = GDN_HEADS
    n_main = GDN_QKV + nh * GDN_HEAD_DIM

    def lane_slot(x, offset):
        return jnp.pad(x, [(0, 0), (offset, LANES - offset - nh)])[:, None, :]

    def sublane_slot(x, offset):
        return jnp.pad(x, [(0, 0), (offset, 2 * nh - offset - nh)])[:, :, None]

    row3 = lambda x: x[:, None, :]
    p = dict(
        norm_g=norm_g,
        ffn_w_in=ffn_w_in.astype(BF16),
        ffn_w_out=ffn_w_out.astype(BF16),
        gdn_w_main=gdn_w_in[:, :, :n_main].astype(BF16),
        gdn_w_ba=jnp.pad(gdn_w_in[:, :, n_main:], [(0, 0), (0, 0), (0, LANES - 2 * nh)]).astype(BF16),
        gdn_conv_w=gdn_conv_w,
        gdn_alr=lane_slot(gdn_a_log, nh), gdn_dbr=lane_slot(gdn_dt_bias, nh),
        gdn_alc=sublane_slot(gdn_a_log, nh), gdn_dbc=sublane_slot(gdn_dt_bias, nh),
        gdn_o_norm=row3(gdn_o_norm),
        gdn_w_out=gdn_w_out.astype(BF16),
        rwkv_lerp=jnp.pad(rwkv_lerp, [(0, 0), (0, SUBLANES - rwkv_lerp.shape[1]), (0, 0)]),
        rwkv_w_r=rwkv_w_rkv[:, 0].astype(BF16), rwkv_w_k=rwkv_w_rkv[:, 1].astype(BF16),
        rwkv_w_v=rwkv_w_rkv[:, 2].astype(BF16),
        rwkv_w0=row3(rwkv_w0), rwkv_w1=rwkv_w1.astype(BF16), rwkv_w2=rwkv_w2.astype(BF16),
        rwkv_a0=row3(rwkv_a0), rwkv_a1=rwkv_a1.astype(BF16), rwkv_a2=rwkv_a2.astype(BF16),
        rwkv_v0=row3(rwkv_v0), rwkv_v1=rwkv_v1.astype(BF16), rwkv_v2=rwkv_v2.astype(BF16),
        rwkv_g1=rwkv_g1.astype(BF16), rwkv_g2=rwkv_g2.astype(BF16),
        rwkv_k_k=row3(rwkv_k_k), rwkv_k_a=row3(rwkv_k_a),
        rwkv_r_k=rwkv_r_k.reshape(n_rwkv, 1, d),
        rwkv_ln_w=row3(rwkv_ln_w), rwkv_ln_b=row3(rwkv_ln_b),
        rwkv_w_o=rwkv_w_o.astype(BF16),
    )
    dtype = x_prompt.dtype
    zero_gdn = jnp.zeros((n_gdn, bsz) + state_gdn.shape[2:], dtype)
    zero_conv = jnp.zeros((n_gdn, bsz) + state_gdn_conv.shape[2:], dtype)
    zero_rwkv = jnp.zeros((n_rwkv, bsz) + state_rwkv.shape[2:], dtype)
    zero_shift = jnp.zeros((n_rwkv, bsz, d), dtype)
    prompt = _trunk(x_prompt.reshape(bsz * t, d), bsz, t, zero_gdn, zero_conv, zero_rwkv, zero_shift, p, chunk=64)
    sample = _trunk(x_sample.reshape(dbsz * dt, d), dbsz, dt, state_gdn, state_gdn_conv, state_rwkv,
                    state_rwkv_shift, p, chunk=SUBLANES)
    return (prompt[0], sample[0]) + prompt[1:] + sample[1:]
```

```python
import functools

import jax
import jax.numpy as jnp
from jax import lax
from jax.experimental import pallas as pl
from jax.experimental.pallas import tpu as pltpu

F32 = jnp.float32
BF16 = jnp.bfloat16

NORM_EPS = 1e-6
L2_EPS = 1e-6
GN_EPS = 64e-5

GDN_HEADS = 8
GDN_HEAD_DIM = 128
GDN_QKV = 3 * GDN_HEADS * GDN_HEAD_DIM
GDN_CONV = 4
RWKV_HEAD = 64

V7X_VMEM_BYTES = 64 * 1024 * 1024
VMEM_LIMIT_BYTES = V7X_VMEM_BYTES * 3 // 4
SUBLANES = 8
LANES = 128
BF16_ROWS = 16
MXU_TILE = 256


def _params(*semantics):
    return pltpu.CompilerParams(dimension_semantics=semantics, vmem_limit_bytes=VMEM_LIMIT_BYTES)


def _rms(x, g):
    return x * lax.rsqrt(jnp.mean(x * x, axis=-1, keepdims=True) + NORM_EPS) * g


def _softplus(x):
    return jnp.maximum(x, 0.0) + jnp.log1p(jnp.exp(-jnp.abs(x)))


def _dot(a, b):
    return jnp.dot(a.astype(BF16), b.astype(BF16), preferred_element_type=F32)


def _dot_nt(a, b):
    return lax.dot_general(a.astype(BF16), b.astype(BF16), (((1,), (1,)), ((), ())),
                           preferred_element_type=F32)


def _dot_tn(a, b):
    return lax.dot_general(a.astype(BF16), b.astype(BF16), (((0,), (0,)), ((), ())),
                           preferred_element_type=F32)


def _split2(a):
    hi = a.astype(BF16)
    lo = (a - hi.astype(F32)).astype(BF16)
    return hi, lo


def _dot_hi(a, b):
    ah, al = _split2(a)
    bh, bl = _split2(b)
    return (jnp.dot(ah, bh, preferred_element_type=F32)
            + (jnp.dot(ah, bl, preferred_element_type=F32) + jnp.dot(al, bh, preferred_element_type=F32)))


def _split3(a):
    hi = a.astype(BF16)
    r = a - hi.astype(F32)
    mid = r.astype(BF16)
    lo = (r - mid.astype(F32)).astype(BF16)
    return hi, mid, lo


def _tri_incl(c):
    ri = lax.broadcasted_iota(jnp.int32, (c, c), 0)
    ci = lax.broadcasted_iota(jnp.int32, (c, c), 1)
    return ri >= ci, ri > ci


def _cumsum_rows(tri_bf, x):
    hi, mid, lo = _split3(x)
    return (jnp.dot(tri_bf, hi, preferred_element_type=F32)
            + (jnp.dot(tri_bf, mid, preferred_element_type=F32) + jnp.dot(tri_bf, lo, preferred_element_type=F32)))


def _cumsum_cols(tri_bf, x):
    hi, mid, lo = _split3(x)
    dn = (((1,), (1,)), ((), ()))
    return (lax.dot_general(hi, tri_bf, dn, preferred_element_type=F32)
            + (lax.dot_general(mid, tri_bf, dn, preferred_element_type=F32)
               + lax.dot_general(lo, tri_bf, dn, preferred_element_type=F32)))


def _unit_lower_inverse(a, c):
    ri = lax.broadcasted_iota(jnp.int32, (c, c), 0)
    ci = lax.broadcasted_iota(jnp.int32, (c, c), 1)
    eye = (ri == ci).astype(F32)
    n = jnp.where((ri >> 3) == (ci >> 3), -a, 0.0)
    p = eye + n
    n2 = _dot_hi(n, n)
    p = p + _dot_hi(p, n2)
    n4 = _dot_hi(n2, n2)
    p = p + _dot_hi(p, n4)
    s = SUBLANES
    while s < c:
        e = _dot_hi(_dot_hi(p, a), p)
        shift = (2 * s).bit_length() - 1
        merge = ((ri >> shift) == (ci >> shift)) & ((ri & s) != 0) & ((ci & s) == 0)
        p = p - jnp.where(merge, e, 0.0)
        s *= 2
    return p


def _log2(n):
    assert n & (n - 1) == 0, n
    return n.bit_length() - 1


def _block_mask(rows, cols, rblk, cblk):
    ri = lax.broadcasted_iota(jnp.int32, (rows, cols), 0)
    ci = lax.broadcasted_iota(jnp.int32, (rows, cols), 1)
    return (ri >> _log2(rblk)) == (ci >> _log2(cblk))


def _tile_rows(x, n):
    return x if n == 1 else jnp.concatenate([x] * n, axis=0)


def _bd(x, n, mask):
    if x.shape[0] % BF16_ROWS == 0:
        return jnp.where(mask, _tile_rows(x.astype(BF16), n), 0)
    return jnp.where(mask, _tile_rows(x.astype(F32), n), 0.0).astype(BF16)


def _wide_products(xs, q, n, mask):
    m = xs[0].shape[0]
    qh, ql = _split2(q)
    bdh = _bd(qh, n, mask)
    bdl = _bd(ql, n, mask)
    parts = [_split2(x) for x in xs]
    k = len(xs)
    if m % BF16_ROWS == 0:
        stacked = jnp.concatenate([p[0] for p in parts] + [p[1] for p in parts], axis=0)
        t = jnp.dot(stacked, bdh, preferred_element_type=F32)
        s = jnp.dot(stacked[:k * m], bdl, preferred_element_type=F32)
        return [t[i * m:(i + 1) * m] + t[(k + i) * m:(k + i + 1) * m] + s[i * m:(i + 1) * m] for i in range(k)]
    return [jnp.dot(ph, bdh, preferred_element_type=F32)
            + (jnp.dot(pl_, bdh, preferred_element_type=F32) + jnp.dot(ph, bdl, preferred_element_type=F32))
            for ph, pl_ in parts]


def _unit_lower_inverse_wide(mats, c, n, mask):
    ri = lax.broadcasted_iota(jnp.int32, (c, n * c), 0)
    cj = lax.broadcasted_iota(jnp.int32, (c, n * c), 1) & (c - 1)
    eye = (ri == cj).astype(F32)
    base = (ri >> 3) == (cj >> 3)
    idx = range(len(mats))
    nm = [jnp.where(base, -a, 0.0) for a in mats]
    p = [eye + x for x in nm]
    n2 = [_wide_products([nm[i]], nm[i], n, mask)[0] for i in idx]
    pn = [_wide_products([p[i], n2[i]], n2[i], n, mask) for i in idx]
    p = [p[i] + pn[i][0] for i in idx]
    p = [p[i] + _wide_products([p[i]], pn[i][1], n, mask)[0] for i in idx]
    s = SUBLANES
    while s < c:
        pa = [_wide_products([p[i]], mats[i], n, mask)[0] for i in idx]
        e = [_wide_products([pa[i]], p[i], n, mask)[0] for i in idx]
        shift = _log2(2 * s)
        merge = ((ri >> shift) == (cj >> shift)) & ((ri & s) != 0) & ((cj & s) == 0)
        p = [p[i] - jnp.where(merge, e[i], 0.0) for i in idx]
        s *= 2
    return p


def _ffn_body(h_ref, gin_ref, gout_ref, gnext_ref, wg_ref, wu_ref, wo_ref, *rest, emit_u):
    if emit_u:
        hout_ref, u_ref, xn_sc, acc_sc = rest
    else:
        hout_ref, xn_sc, acc_sc = rest
    j = pl.program_id(1)

    @pl.when(j == 0)
    def _():
        xn_sc[...] = _rms(h_ref[...], gin_ref[...]).astype(BF16)
        acc_sc[...] = jnp.zeros_like(acc_sc)

    xn = xn_sc[...]
    gate = jnp.dot(xn, wg_ref[...], preferred_element_type=F32)
    up = jnp.dot(xn, wu_ref[...], preferred_element_type=F32)
    act = (gate * jax.nn.sigmoid(gate)) * up
    acc_sc[...] += jnp.dot(act.astype(BF16), wo_ref[...], preferred_element_type=F32)

    @pl.when(j == pl.num_programs(1) - 1)
    def _():
        hn = h_ref[...] + 0.5 * _rms(acc_sc[...], gout_ref[...])
        hout_ref[...] = hn
        if emit_u:
            u_ref[...] = _rms(hn, gnext_ref[...])


def _ffn(h, g_in, g_out, g_next, w_in, w_out, *, emit_u, tm, tf):
    n, d = h.shape
    f = w_out.shape[0]
    nf = f // tf
    row = pl.BlockSpec((tm, d), lambda i, j: (i, 0))
    vec = pl.BlockSpec((1, d), lambda i, j: (0, 0))
    out_shape = [jax.ShapeDtypeStruct((n, d), F32)]
    out_specs = [row]
    if emit_u:
        out_shape.append(jax.ShapeDtypeStruct((n, d), F32))
        out_specs.append(row)
    return pl.pallas_call(
        functools.partial(_ffn_body, emit_u=emit_u),
        grid=(n // tm, nf),
        in_specs=[row, vec, vec, vec,
                  pl.BlockSpec((d, tf), lambda i, j: (0, j)),
                  pl.BlockSpec((d, tf), lambda i, j: (0, j + nf)),
                  pl.BlockSpec((tf, d), lambda i, j: (j, 0))],
        out_specs=out_specs,
        out_shape=out_shape,
        scratch_shapes=[pltpu.VMEM((tm, d), BF16), pltpu.VMEM((tm, d), F32)],
        compiler_params=_params("parallel", "arbitrary"),
        name="ffn_u" if emit_u else "ffn",
    )(h, g_in, g_out, g_next, w_in, w_in, w_out)


def _gdn_proj_body(u_ref, w_ref, wba_ref, main_ref, ba_ref, xb_sc):
    j = pl.program_id(1)

    @pl.when(j == 0)
    def _():
        xb = u_ref[...].astype(BF16)
        xb_sc[...] = xb
        ba_ref[...] = jnp.dot(xb, wba_ref[...], preferred_element_type=F32)

    main_ref[...] = jnp.dot(xb_sc[...], w_ref[...], preferred_element_type=F32)


def _gdn_proj(u, w_main, w_ba, *, tm, tn):
    n, d = u.shape
    nm = w_main.shape[1]
    return pl.pallas_call(
        _gdn_proj_body,
        grid=(n // tm, nm // tn),
        in_specs=[pl.BlockSpec((tm, d), lambda i, j: (i, 0)),
                  pl.BlockSpec((d, tn), lambda i, j: (0, j)),
                  pl.BlockSpec((d, LANES), lambda i, j: (0, 0))],
        out_specs=[pl.BlockSpec((tm, tn), lambda i, j: (i, j)),
                   pl.BlockSpec((tm, LANES), lambda i, j: (i, 0))],
        out_shape=[jax.ShapeDtypeStruct((n, nm), F32), jax.ShapeDtypeStruct((n, LANES), F32)],
        scratch_shapes=[pltpu.VMEM((tm, d), BF16)],
        compiler_params=_params("parallel", "arbitrary"),
        name="gdn_proj",
    )(u, w_main, w_ba)


def _gdn_rec_body(cur_ref, prev_ref, init_ref, ba_ref, cw_ref, alr_ref, dbr_ref,
                  s0_ref, o_ref, sout_ref, ext_sc, s_sc, *, chunk, t_valid):
    c = pl.program_id(1)
    hd = GDN_HEAD_DIM
    nh = GDN_HEADS

    @pl.when(c == 0)
    def _():
        s_sc[...] = jnp.zeros_like(s_sc)
        for h in range(nh):
            g, j = divmod(h, MXU_TILE // hd)
            s_sc[g, j * hd:(j + 1) * hd, j * hd:(j + 1) * hd] = s0_ref[0, h]
        ext_sc[0:SUBLANES, :] = init_ref[0]

    @pl.when(c > 0)
    def _():
        ext_sc[0:SUBLANES, :] = prev_ref[0]

    ext_sc[SUBLANES:SUBLANES + chunk, :] = cur_ref[0]

    first = SUBLANES - (GDN_CONV - 1)
    conv = cw_ref[0:1, :] * ext_sc[first:first + chunk, :]
    for tap in range(1, GDN_CONV):
        conv = conv + cw_ref[tap:tap + 1, :] * ext_sc[first + tap:first + tap + chunk, :]
    qkv = conv * jax.nn.sigmoid(conv)

    ba = ba_ref[0]
    g_cols = -jnp.exp(alr_ref[...]) * _softplus(ba + dbr_ref[...])
    beta_cols = jax.nn.sigmoid(ba)
    if t_valid < chunk:
        row_ok = lax.broadcasted_iota(jnp.int32, (chunk, 1), 0) < t_valid
        qkv = jnp.where(row_ok, qkv, 0.0)
        g_cols = jnp.where(row_ok, g_cols, 0.0)

    incl, _ = _tri_incl(chunk)
    gc_cols = _cumsum_rows(incl.astype(BF16), g_cols)

    kb, qn, kn, vb, kbeg, qdec, kdec, gcol, dlast = ([] for _ in range(9))
    for h in range(nh):
        q = qkv[:, h * hd:(h + 1) * hd]
        k = qkv[:, (nh + h) * hd:(nh + h + 1) * hd]
        v = qkv[:, (2 * nh + h) * hd:(2 * nh + h + 1) * hd]
        qn_h = q * lax.rsqrt(jnp.sum(q * q, axis=-1, keepdims=True) + L2_EPS) * (hd ** -0.5)
        kn_h = k * lax.rsqrt(jnp.sum(k * k, axis=-1, keepdims=True) + L2_EPS)
        beta = beta_cols[:, h:h + 1]
        gc = gc_cols[:, nh + h:nh + h + 1]
        eg = jnp.exp(gc)
        g_last = gc[chunk - 1:chunk, :]
        kb_h = kn_h * beta
        kb.append(kb_h)
        qn.append(qn_h)
        kn.append(kn_h)
        vb.append(v * beta)
        kbeg.append(kb_h * eg)
        qdec.append(qn_h * eg)
        kdec.append(kn_h * jnp.exp(g_last - gc))
        gcol.append(gc)
        dlast.append(jnp.exp(g_last))

    sgh = MXU_TILE // hd
    nsg = nh // sgh
    hg = min(nh, MXU_TILE // chunk)
    ntg = nh // hg
    wt = hg * chunk
    kt = hg * hd
    sg_per_tg = kt // MXU_TILE
    tgs = range(ntg)
    cat = lambda xs, t: jnp.concatenate(xs[t * hg:(t + 1) * hg], axis=1)
    head_mask = _block_mask(wt, kt, chunk, hd)
    wide_mask = _block_mask(wt, wt, chunk, chunk)
    state_mask = _block_mask(MXU_TILE, MXU_TILE, hd, hd)
    ri = lax.broadcasted_iota(jnp.int32, (chunk, wt), 0)
    cw = lax.broadcasted_iota(jnp.int32, (chunk, wt), 1)
    cj = cw & (chunk - 1)
    strict_w = ri > cj
    incl_w = ri >= cj
    ones_cc = jnp.ones((chunk, chunk), BF16)

    decay = []
    for t in tgs:
        gce = jnp.broadcast_to(gcol[t * hg], (chunk, wt))
        for j in range(1, hg):
            gce = jnp.where(cw >= j * chunk, jnp.broadcast_to(gcol[t * hg + j], (chunk, wt)), gce)
        hi, mid, lo = _split3(jnp.where(ri == cj, gce, 0.0))
        gcj = (jnp.dot(ones_cc, hi, preferred_element_type=F32)
               + (jnp.dot(ones_cc, mid, preferred_element_type=F32) + jnp.dot(ones_cc, lo, preferred_element_type=F32)))
        decay.append(jnp.exp(jnp.where(incl_w, gce - gcj, -jnp.inf)))

    lhs = [jnp.concatenate([cat(kb, t), cat(qn, t)], axis=0).astype(BF16) for t in tgs]
    mk = [lax.dot_general(lhs[t], _bd(cat(kn, t), hg, head_mask), (((1,), (1,)), ((), ())),
                          preferred_element_type=F32) for t in tgs]
    a_mat = [jnp.where(strict_w, mk[t][:chunk] * decay[t], 0.0) for t in tgs]
    attn = [jnp.where(incl_w, mk[t][chunk:] * decay[t], 0.0) for t in tgs]
    t_mat = _unit_lower_inverse_wide(a_mat, chunk, hg, wide_mask)
    t_bf = [x.astype(BF16) for x in t_mat]
    u = [jnp.dot(t_bf[t], _bd(cat(vb, t), hg, head_mask), preferred_element_type=F32) for t in tgs]
    w = [jnp.dot(t_bf[t], _bd(cat(kbeg, t), hg, head_mask), preferred_element_type=F32) for t in tgs]

    v_new, qs = [], []
    for t in tgs:
        qd = cat(qdec, t)
        parts = []
        for q_ in range(sg_per_tg):
            sl = slice(q_ * MXU_TILE, (q_ + 1) * MXU_TILE)
            parts.append(_dot(jnp.concatenate([w[t][:, sl], qd[:, sl]], axis=0), s_sc[t * sg_per_tg + q_]))
        ws = parts[0] if sg_per_tg == 1 else jnp.concatenate(parts, axis=1)
        v_new.append(u[t] - ws[:chunk])
        qs.append(ws[chunk:])
    o = [qs[t] + jnp.dot(attn[t].astype(BF16), _bd(v_new[t], hg, head_mask), preferred_element_type=F32)
         for t in tgs]
    o_ref[0] = o[0] if ntg == 1 else jnp.concatenate(o, axis=1)

    vn_all = v_new[0] if ntg == 1 else jnp.concatenate(v_new, axis=1)
    lane = lax.broadcasted_iota(jnp.int32, (1, MXU_TILE), 1)
    for g in range(nsg):
        sl = slice(g * MXU_TILE, (g + 1) * MXU_TILE)
        dl = jnp.broadcast_to(dlast[g * sgh], (1, MXU_TILE))
        for j in range(1, sgh):
            dl = jnp.where(lane >= j * hd, jnp.broadcast_to(dlast[g * sgh + j], (1, MXU_TILE)), dl)
        upd = _dot_tn(jnp.concatenate(kdec[g * sgh:(g + 1) * sgh], axis=1), vn_all[:, sl])
        s_sc[g] = s_sc[g] * dl + jnp.where(state_mask, upd, 0.0)

    @pl.when(c == pl.num_programs(1) - 1)
    def _():
        for h in range(nh):
            g, j = divmod(h, sgh)
            sout_ref[0, h] = s_sc[g, j * hd:(j + 1) * hd, j * hd:(j + 1) * hd]


def _gdn_rec(qkv, conv_init, ba, conv_w, alr, dbr, s0, *, chunk, t_valid):
    bsz, tp, width = qkv.shape
    nc = tp // chunk
    cb = chunk // SUBLANES
    dv = GDN_HEADS * GDN_HEAD_DIM
    full2 = lambda shape: pl.BlockSpec(shape, lambda b, c: (0, 0))
    return pl.pallas_call(
        functools.partial(_gdn_rec_body, chunk=chunk, t_valid=t_valid),
        grid=(bsz, nc),
        in_specs=[pl.BlockSpec((1, chunk, GDN_QKV), lambda b, c: (b, c, 0)),
                  pl.BlockSpec((1, SUBLANES, GDN_QKV), lambda b, c: (b, jnp.maximum(c * cb - 1, 0), 0)),
                  pl.BlockSpec((1, SUBLANES, GDN_QKV), lambda b, c: (b, 0, 0)),
                  pl.BlockSpec((1, chunk, LANES), lambda b, c: (b, c, 0)),
                  full2(conv_w.shape), full2(alr.shape), full2(dbr.shape),
                  pl.BlockSpec((1, GDN_HEADS, GDN_HEAD_DIM, GDN_HEAD_DIM), lambda b, c: (b, 0, 0, 0))],
        out_specs=[pl.BlockSpec((1, chunk, dv), lambda b, c: (b, c, 0)),
                   pl.BlockSpec((1, GDN_HEADS, GDN_HEAD_DIM, GDN_HEAD_DIM), lambda b, c: (b, 0, 0, 0))],
        out_shape=[jax.ShapeDtypeStruct((bsz, tp, dv), F32),
                   jax.ShapeDtypeStruct((bsz, GDN_HEADS, GDN_HEAD_DIM, GDN_HEAD_DIM), F32)],
        scratch_shapes=[pltpu.VMEM((chunk + SUBLANES, GDN_QKV), F32),
                        pltpu.VMEM((dv // MXU_TILE, MXU_TILE, MXU_TILE), F32)],
        compiler_params=_params("parallel", "arbitrary"),
        name="gdn_rec",
    )(qkv, qkv, conv_init, ba, conv_w, alr, dbr, s0)


def _gdn_post_body(o_ref, z_ref, h_ref, on_ref, w_ref, g_ref, hout_ref, gated_sc):
    hd = GDN_HEAD_DIM
    for h in range(GDN_HEADS):
        sl = slice(h * hd, (h + 1) * hd)
        o = o_ref[:, sl]
        z = z_ref[:, sl]
        on = o * lax.rsqrt(jnp.mean(o * o, axis=-1, keepdims=True) + NORM_EPS) * on_ref[...]
        gated_sc[:, sl] = (on * (z * jax.nn.sigmoid(z))).astype(BF16)
    y = jnp.dot(gated_sc[...], w_ref[...], preferred_element_type=F32)
    hout_ref[...] = h_ref[...] + _rms(y, g_ref[...])


def _gdn_post(o, main, h, o_norm, w_out, g, *, tm):
    n, d = h.shape
    dv = o.shape[1]
    zblk = GDN_QKV // dv
    row = pl.BlockSpec((tm, d), lambda i: (i, 0))
    return pl.pallas_call(
        _gdn_post_body,
        grid=(n // tm,),
        in_specs=[pl.BlockSpec((tm, dv), lambda i: (i, 0)),
                  pl.BlockSpec((tm, dv), lambda i: (i, zblk)),
                  row,
                  pl.BlockSpec((1, GDN_HEAD_DIM), lambda i: (0, 0)),
                  pl.BlockSpec((dv, d), lambda i: (0, 0)),
                  pl.BlockSpec((1, d), lambda i: (0, 0))],
        out_specs=row,
        out_shape=jax.ShapeDtypeStruct((n, d), F32),
        scratch_shapes=[pltpu.VMEM((tm, dv), BF16)],
        compiler_params=_params("parallel"),
        name="gdn_post",
    )(o, main, h, o_norm, w_out, g)


def _rwkv_proj_body(*refs, has_vres):
    if has_vres:
        (x_ref, xp_ref, vf_ref, lerp_ref, wr_ref, wk_ref, wv_ref, w0_ref, w1_ref, w2_ref, a0_ref, a1_ref,
         a2_ref, g1_ref, g2_ref, v0_ref, v1_ref, v2_ref, r_ref, k_ref, v_ref, lw_ref, a_ref, g_ref) = refs
    else:
        (x_ref, xp_ref, lerp_ref, wr_ref, wk_ref, wv_ref, w0_ref, w1_ref, w2_ref, a0_ref, a1_ref,
         a2_ref, g1_ref, g2_ref, r_ref, k_ref, v_ref, lw_ref, a_ref, g_ref) = refs
    x = x_ref[...]
    dx = xp_ref[...] - x

    def mix(s):
        return (x + dx * lerp_ref[s:s + 1, :]).astype(BF16)

    r_ref[...] = jnp.dot(mix(0), wr_ref[...], preferred_element_type=F32)
    k_ref[...] = jnp.dot(mix(1), wk_ref[...], preferred_element_type=F32)
    xv = mix(2)
    v = jnp.dot(xv, wv_ref[...], preferred_element_type=F32)
    if has_vres:
        gate = jax.nn.sigmoid(v0_ref[...] + _dot(jnp.dot(xv, v1_ref[...], preferred_element_type=F32), v2_ref[...]))
        v = v + (vf_ref[...] - v) * gate
    v_ref[...] = v
    dec = w0_ref[...] + _dot(jnp.tanh(jnp.dot(mix(3), w1_ref[...], preferred_element_type=F32)), w2_ref[...])
    logw = -_softplus(-dec) - 0.5
    lw_ref[...] = -jnp.exp(logw)
    a_ref[...] = jax.nn.sigmoid(a0_ref[...] + _dot(jnp.dot(mix(4), a1_ref[...], preferred_element_type=F32),
                                                   a2_ref[...]))
    g_ref[...] = _dot(jax.nn.sigmoid(jnp.dot(mix(5), g1_ref[...], preferred_element_type=F32)), g2_ref[...])


def _rwkv_proj(x, xp, v_first, lerp, wr, wk, wv, w0, w1, w2, a0, a1, a2, g1, g2, vres, *, tm):
    n, d = x.shape
    has_vres = vres is not None
    row = pl.BlockSpec((tm, d), lambda i: (i, 0))
    full = lambda a: pl.BlockSpec(a.shape, lambda i: (0, 0))
    args = [x, xp] + ([v_first] if has_vres else []) + [lerp, wr, wk, wv, w0, w1, w2, a0, a1, a2, g1, g2]
    in_specs = [row, row] + ([row] if has_vres else []) + [full(a) for a in args[(3 if has_vres else 2):]]
    if has_vres:
        args += list(vres)
        in_specs += [full(a) for a in vres]
    return pl.pallas_call(
        functools.partial(_rwkv_proj_body, has_vres=has_vres),
        grid=(n // tm,),
        in_specs=in_specs,
        out_specs=[row] * 6,
        out_shape=[jax.ShapeDtypeStruct((n, d), F32)] * 6,
        compiler_params=_params("parallel"),
        name="rwkv_proj_vres" if has_vres else "rwkv_proj",
    )(*args)


def _rwkv_rec_body(r_ref, k_ref, v_ref, lw_ref, a_ref, kk_ref, ka_ref, rk_ref, lnw_ref, lnb_ref, s0_ref,
                   y_ref, sout_ref, s_sc, *, chunk, t_valid, heads):
    c = pl.program_id(1)
    hd = RWKV_HEAD
    d = heads * hd
    gw = MXU_TILE
    sgh = gw // hd
    nsg = heads // sgh
    hg = min(heads, gw // chunk)
    ntg = heads // hg
    kt = hg * hd
    sg_per_tg = kt // gw

    @pl.when(c == 0)
    def _():
        s_sc[...] = jnp.zeros_like(s_sc)
        for h in range(heads):
            g, j = divmod(h, sgh)
            s_sc[g, j * hd:(j + 1) * hd, j * hd:(j + 1) * hd] = s0_ref[0, h]

    r_all = r_ref[0]
    k_all = k_ref[0]
    v_all = v_ref[0]
    lw_all = lw_ref[0]
    a_all = a_ref[0]
    if t_valid < chunk:
        row_ok = lax.broadcasted_iota(jnp.int32, (chunk, 1), 0) < t_valid
        r_all = jnp.where(row_ok, r_all, 0.0)
        k_all = jnp.where(row_ok, k_all, 0.0)
        v_all = jnp.where(row_ok, v_all, 0.0)
        lw_all = jnp.where(row_ok, lw_all, 0.0)

    state_mask = _block_mask(gw, gw, hd, hd)
    ones_bd = state_mask.astype(BF16)

    def head_sum(x):
        outs = []
        for g in range(nsg):
            hi, lo = _split2(x[:, g * gw:(g + 1) * gw])
            outs.append(jnp.dot(hi, ones_bd, preferred_element_type=F32)
                        + jnp.dot(lo, ones_bd, preferred_element_type=F32))
        return jnp.concatenate(outs, axis=1)

    incl, _ = _tri_incl(chunk)
    gc = _cumsum_rows(incl.astype(BF16), lw_all)
    kx = k_all * kk_ref[...]
    kk = kx * lax.rsqrt(head_sum(kx * kx) + L2_EPS)
    kmod = k_all * (1.0 + (a_all - 1.0) * ka_ref[...])
    bv = kk * a_all
    e_neg = jnp.exp(-gc)
    at = -kk * jnp.exp(gc - lw_all)
    rt = r_all * jnp.exp(gc)
    bt = bv * e_neg
    ktl = kmod * e_neg
    g_last = gc[chunk - 1:chunk, :]
    e_rem = jnp.exp(g_last - gc)
    bp = bv * e_rem
    kp = kmod * e_rem
    dec = jnp.exp(g_last)

    head_mask = _block_mask(hg * chunk, kt, chunk, hd)
    wide_mask = _block_mask(hg * chunk, hg * chunk, chunk, chunk)
    ri = lax.broadcasted_iota(jnp.int32, (chunk, hg * chunk), 0)
    cj = lax.broadcasted_iota(jnp.int32, (chunk, hg * chunk), 1) & (chunk - 1)
    strict_w = ri > cj
    incl_w = ri >= cj

    tgs = range(ntg)
    lane = lambda x, t: x[:, t * kt:(t + 1) * kt]
    lhs = [jnp.concatenate([lane(at, t), lane(rt, t)], axis=0).astype(BF16) for t in tgs]
    mb = [lax.dot_general(lhs[t], _bd(lane(bt, t), hg, head_mask), (((1,), (1,)), ((), ())),
                          preferred_element_type=F32) for t in tgs]
    mk = [lax.dot_general(lhs[t], _bd(lane(ktl, t), hg, head_mask), (((1,), (1,)), ((), ())),
                          preferred_element_type=F32) for t in tgs]
    a_ab = [jnp.where(strict_w, mb[t][:chunk], 0.0) for t in tgs]
    a_rb = [jnp.where(incl_w, mb[t][chunk:], 0.0) for t in tgs]
    a_ak = [jnp.where(strict_w, mk[t][:chunk], 0.0) for t in tgs]
    a_rk = [jnp.where(incl_w, mk[t][chunk:], 0.0) for t in tgs]
    t_mat = _unit_lower_inverse_wide([-x for x in a_ab], chunk, hg, wide_mask)

    xs = []
    for t in tgs:
        parts = []
        for q in range(sg_per_tg):
            g = t * sg_per_tg + q
            parts.append(lax.dot_general(lhs[t][:, q * gw:(q + 1) * gw], s_sc[g].astype(BF16),
                                         (((1,), (1,)), ((), ())), preferred_element_type=F32))
        xs.append(parts[0] if len(parts) == 1 else jnp.concatenate(parts, axis=1))
    bd_v = [_bd(lane(v_all, t), hg, head_mask) for t in tgs]
    rhs_u = [xs[t][:chunk] + jnp.dot(a_ak[t].astype(BF16), bd_v[t], preferred_element_type=F32) for t in tgs]
    u = [jnp.dot(t_mat[t].astype(BF16), _bd(rhs_u[t], hg, head_mask), preferred_element_type=F32) for t in tgs]
    y = [xs[t][chunk:]
         + jnp.dot(a_rb[t].astype(BF16), _bd(u[t], hg, head_mask), preferred_element_type=F32)
         + jnp.dot(a_rk[t].astype(BF16), bd_v[t], preferred_element_type=F32) for t in tgs]

    u_all = u[0] if ntg == 1 else jnp.concatenate(u, axis=1)
    for g in range(nsg):
        sl = slice(g * gw, (g + 1) * gw)
        upd = _dot_tn(jnp.concatenate([u_all[:, sl], v_all[:, sl]], axis=0),
                      jnp.concatenate([bp[:, sl], kp[:, sl]], axis=0))
        s_sc[g] = s_sc[g] * dec[:, sl] + jnp.where(state_mask, upd, 0.0)

    y_all = y[0] if ntg == 1 else jnp.concatenate(y, axis=1)
    yc = y_all - head_sum(y_all) * (1.0 / hd)
    yn = yc * lax.rsqrt(head_sum(yc * yc) * (1.0 / hd) + GN_EPS)
    bonus = head_sum(r_all * kmod * rk_ref[...]) * v_all
    y_ref[0] = yn * lnw_ref[...] + lnb_ref[...] + bonus

    @pl.when(c == pl.num_programs(1) - 1)
    def _():
        for h in range(heads):
            g, j = divmod(h, sgh)
            sout_ref[0, h] = s_sc[g, j * hd:(j + 1) * hd, j * hd:(j + 1) * hd]


def _rwkv_rec(r, k, v, lw, a, k_k, k_a, r_k, ln_w, ln_b, s0, *, chunk, t_valid):
    bsz, tp, d = r.shape
    heads = d // RWKV_HEAD
    nc = tp // chunk
    seq = pl.BlockSpec((1, chunk, d), lambda b, c: (b, c, 0))
    vec = pl.BlockSpec((1, d), lambda b, c: (0, 0))
    st = pl.BlockSpec((1, heads, RWKV_HEAD, RWKV_HEAD), lambda b, c: (b, 0, 0, 0))
    return pl.pallas_call(
        functools.partial(_rwkv_rec_body, chunk=chunk, t_valid=t_valid, heads=heads),
        grid=(bsz, nc),
        in_specs=[seq] * 5 + [vec] * 5 + [st],
        out_specs=[seq, st],
        out_shape=[jax.ShapeDtypeStruct((bsz, tp, d), F32),
                   jax.ShapeDtypeStruct((bsz, heads, RWKV_HEAD, RWKV_HEAD), F32)],
        scratch_shapes=[pltpu.VMEM((d // MXU_TILE, MXU_TILE, MXU_TILE), F32)],
        compiler_params=_params("parallel", "arbitrary"),
        name="rwkv_rec",
    )(r, k, v, lw, a, k_k, k_a, r_k, ln_w, ln_b, s0)


def _rwkv_post_body(y_ref, gate_ref, h_ref, w_ref, g_ref, hout_ref):
    out = jnp.dot((y_ref[...] * gate_ref[...]).astype(BF16), w_ref[...], preferred_element_type=F32)
    hout_ref[...] = h_ref[...] + _rms(out, g_ref[...])


def _rwkv_post(y, gate, h, w_o, g, *, tm):
    n, d = h.shape
    row = pl.BlockSpec((tm, d), lambda i: (i, 0))
    return pl.pallas_call(
        _rwkv_post_body,
        grid=(n // tm,),
        in_specs=[row, row, row, pl.BlockSpec((d, d), lambda i: (0, 0)), pl.BlockSpec((1, d), lambda i: (0, 0))],
        out_specs=row,
        out_shape=jax.ShapeDtypeStruct((n, d), F32),
        compiler_params=_params("parallel"),
        name="rwkv_post",
    )(y, gate, h, w_o, g)


def _pad_time(x, tp):
    t = x.shape[1]
    if t == tp:
        return x
    return jnp.pad(x, [(0, 0), (0, tp - t)] + [(0, 0)] * (x.ndim - 2))


def _tiles(n):
    return 1024 if n % 1024 == 0 else n


def _trunk(h, bsz, t, gdn_s, gdn_buf, rw_s, rw_shift, p, chunk):
    n, d = h.shape
    depth = p["norm_g"].shape[0]
    tm = _tiles(n)
    tp = -(-t // chunk) * chunk
    out_gdn_s, out_gdn_buf, out_rw_s, out_rw_shift = [], [], [], []
    v_first = None
    for i in range(depth):
        ng = p["norm_g"][i]
        row = lambda s: ng[s][None, :]
        h, u = _ffn(h, row(0), row(1), row(2), p["ffn_w_in"][i][0], p["ffn_w_out"][i][0],
                    emit_u=True, tm=tm, tf=256)
        j = i // 2
        if i % 2 == 0:
            main, ba = _gdn_proj(u, p["gdn_w_main"][j], p["gdn_w_ba"][j], tm=tm, tn=1024)
            main3 = main.reshape(bsz, t, main.shape[1])
            out_gdn_buf.append(main3[:, t - (GDN_CONV - 1):, :GDN_QKV])
            ba3 = _pad_time(ba.reshape(bsz, t, LANES), tp)
            conv_init = jnp.pad(gdn_buf[j], [(0, 0), (SUBLANES - (GDN_CONV - 1), 0), (0, 0)])
            qkv = main3 if tp == t else _pad_time(main3[:, :, :GDN_QKV], tp)
            o, s = _gdn_rec(qkv, conv_init, ba3, p["gdn_conv_w"][j], p["gdn_alr"][j], p["gdn_dbr"][j],
                            gdn_s[j], chunk=chunk, t_valid=min(t, chunk))
            out_gdn_s.append(s)
            o2 = o[:, :t].reshape(n, o.shape[2])
            h = _gdn_post(o2, main, h, p["gdn_o_norm"][j], p["gdn_w_out"][j], row(3), tm=min(tm, 512))
        else:
            u3 = u.reshape(bsz, t, d)
            out_rw_shift.append(u3[:, -1])
            xp = jnp.concatenate([rw_shift[j][:, None], u3[:, :-1]], axis=1).reshape(n, d)
            vres = None if j == 0 else (p["rwkv_v0"][j - 1], p["rwkv_v1"][j - 1], p["rwkv_v2"][j - 1])
            r, k, v, lw, a, g = _rwkv_proj(
                u, xp, v_first, p["rwkv_lerp"][j], p["rwkv_w_r"][j], p["rwkv_w_k"][j], p["rwkv_w_v"][j],
                p["rwkv_w0"][j], p["rwkv_w1"][j], p["rwkv_w2"][j], p["rwkv_a0"][j], p["rwkv_a1"][j],
                p["rwkv_a2"][j], p["rwkv_g1"][j], p["rwkv_g2"][j], vres, tm=min(tm, 256))
            if v_first is None:
                v_first = v
            seq = lambda x: _pad_time(x.reshape(bsz, t, d), tp)
            y, s = _rwkv_rec(seq(r), seq(k), seq(v), seq(lw), seq(a), p["rwkv_k_k"][j], p["rwkv_k_a"][j],
                             p["rwkv_r_k"][j], p["rwkv_ln_w"][j], p["rwkv_ln_b"][j], rw_s[j],
                             chunk=chunk, t_valid=min(t, chunk))
            out_rw_s.append(s)
            h = _rwkv_post(y[:, :t].reshape(n, d), g, h, p["rwkv_w_o"][j], row(3), tm=min(tm, 512))
        h = _ffn(h, row(4), row(5), row(5), p["ffn_w_in"][i][1], p["ffn_w_out"][i][1],
                 emit_u=False, tm=tm, tf=256)[0]
    return (h.reshape(bsz, t, d), jnp.stack(out_gdn_s), jnp.stack(out_gdn_buf), jnp.stack(out_rw_s),
            jnp.stack(out_rw_shift))


def kernel(x_prompt, x_sample, state_gdn, state_gdn_conv, state_rwkv, state_rwkv_shift, norm_g, ffn_w_in, ffn_w_out, gdn_w_in, gdn_conv_w, gdn_a_log, gdn_dt_bias, gdn_o_norm, gdn_w_out, rwkv_lerp, rwkv_w_rkv, rwkv_w0, rwkv_w1, rwkv_w2, rwkv_a0, rwkv_a1, rwkv_a2, rwkv_v0, rwkv_v1, rwkv_v2, rwkv_g1, rwkv_g2, rwkv_k_k, rwkv_k_a, rwkv_r_k, rwkv_ln_w, rwkv_ln_b, rwkv_w_o):
    bsz, t, d = x_prompt.shape
    dbsz, dt, _ = x_sample.shape
    n_gdn = gdn_w_in.shape[0]
    n_rwkv = rwkv_w_rkv.shape[0]
    nh = GDN_HEADS
    n_main = GDN_QKV + nh * GDN_HEAD_DIM

    def lane_slot(x, offset):
        return jnp.pad(x, [(0, 0), (offset, LANES - offset - nh)])[:, None, :]

    row3 = lambda x: x[:, None, :]
    p = dict(
        norm_g=norm_g,
        ffn_w_in=ffn_w_in.astype(BF16),
        ffn_w_out=ffn_w_out.astype(BF16),
        gdn_w_main=gdn_w_in[:, :, :n_main].astype(BF16),
        gdn_w_ba=jnp.pad(gdn_w_in[:, :, n_main:], [(0, 0), (0, 0), (0, LANES - 2 * nh)]).astype(BF16),
        gdn_conv_w=gdn_conv_w,
        gdn_alr=lane_slot(gdn_a_log, nh), gdn_dbr=lane_slot(gdn_dt_bias, nh),
        gdn_o_norm=row3(gdn_o_norm),
        gdn_w_out=gdn_w_out.astype(BF16),
        rwkv_lerp=jnp.pad(rwkv_lerp, [(0, 0), (0, SUBLANES - rwkv_lerp.shape[1]), (0, 0)]),
        rwkv_w_r=rwkv_w_rkv[:, 0].astype(BF16), rwkv_w_k=rwkv_w_rkv[:, 1].astype(BF16),
        rwkv_w_v=rwkv_w_rkv[:, 2].astype(BF16),
        rwkv_w0=row3(rwkv_w0), rwkv_w1=rwkv_w1.astype(BF16), rwkv_w2=rwkv_w2.astype(BF16),
        rwkv_a0=row3(rwkv_a0), rwkv_a1=rwkv_a1.astype(BF16), rwkv_a2=rwkv_a2.astype(BF16),
        rwkv_v0=row3(rwkv_v0), rwkv_v1=rwkv_v1.astype(BF16), rwkv_v2=rwkv_v2.astype(BF16),
        rwkv_g1=rwkv_g1.astype(BF16), rwkv_g2=rwkv_g2.astype(BF16),
        rwkv_k_k=row3(rwkv_k_k), rwkv_k_a=row3(rwkv_k_a),
        rwkv_r_k=rwkv_r_k.reshape(n_rwkv, 1, d),
        rwkv_ln_w=row3(rwkv_ln_w), rwkv_ln_b=row3(rwkv_ln_b),
        rwkv_w_o=rwkv_w_o.astype(BF16),
    )
    dtype = x_prompt.dtype
    zero_gdn = jnp.zeros((n_gdn, bsz) + state_gdn.shape[2:], dtype)
    zero_conv = jnp.zeros((n_gdn, bsz) + state_gdn_conv.shape[2:], dtype)
    zero_rwkv = jnp.zeros((n_rwkv, bsz) + state_rwkv.shape[2:], dtype)
    zero_shift = jnp.zeros((n_rwkv, bsz, d), dtype)
    prompt = _trunk(x_prompt.reshape(bsz * t, d), bsz, t, zero_gdn, zero_conv, zero_rwkv, zero_shift, p, chunk=64)
    sample = _trunk(x_sample.reshape(dbsz * dt, d), dbsz, dt, state_gdn, state_gdn_conv, state_rwkv,
                    state_rwkv_shift, p, chunk=SUBLANES)
    return (prompt[0], sample[0]) + prompt[1:] + sample[1:]
```

```python
import functools

import jax
import jax.numpy as jnp
from jax import lax
from jax.experimental import pallas as pl
from jax.experimental.pallas import tpu as pltpu

F32 = jnp.float32
BF16 = jnp.bfloat16

NORM_EPS = 1e-6
L2_EPS = 1e-6
GN_EPS = 64e-5

GDN_HEADS = 8
GDN_HEAD_DIM = 128
GDN_QKV = 3 * GDN_HEADS * GDN_HEAD_DIM
GDN_CONV = 4
RWKV_HEAD = 64

V7X_VMEM_BYTES = 64 * 1024 * 1024
VMEM_LIMIT_BYTES = V7X_VMEM_BYTES * 3 // 4
SUBLANES = 8
LANES = 128
BF16_ROWS = 16
MXU_TILE = 256


def _params(*semantics):
    return pltpu.CompilerParams(dimension_semantics=semantics, vmem_limit_bytes=VMEM_LIMIT_BYTES)


def _rms(x, g):
    return x * lax.rsqrt(jnp.mean(x * x, axis=-1, keepdims=True) + NORM_EPS) * g


def _softplus(x):
    return jnp.maximum(x, 0.0) + jnp.log1p(jnp.exp(-jnp.abs(x)))


def _dot(a, b):
    return jnp.dot(a.astype(BF16), b.astype(BF16), preferred_element_type=F32)


def _dot_tn(a, b):
    return lax.dot_general(a.astype(BF16), b.astype(BF16), (((0,), (0,)), ((), ())),
                           preferred_element_type=F32)


def _split2(a):
    hi = a.astype(BF16)
    lo = (a - hi.astype(F32)).astype(BF16)
    return hi, lo


def _split3(a):
    hi = a.astype(BF16)
    r = a - hi.astype(F32)
    mid = r.astype(BF16)
    lo = (r - mid.astype(F32)).astype(BF16)
    return hi, mid, lo


def _tri_incl(c):
    ri = lax.broadcasted_iota(jnp.int32, (c, c), 0)
    ci = lax.broadcasted_iota(jnp.int32, (c, c), 1)
    return ri >= ci, ri > ci


def _cumsum_rows(tri_bf, x):
    hi, mid, lo = _split3(x)
    return (jnp.dot(tri_bf, hi, preferred_element_type=F32)
            + (jnp.dot(tri_bf, mid, preferred_element_type=F32) + jnp.dot(tri_bf, lo, preferred_element_type=F32)))


def _log2(n):
    assert n & (n - 1) == 0, n
    return n.bit_length() - 1


def _block_mask(rows, cols, rblk, cblk):
    ri = lax.broadcasted_iota(jnp.int32, (rows, cols), 0)
    ci = lax.broadcasted_iota(jnp.int32, (rows, cols), 1)
    return (ri >> _log2(rblk)) == (ci >> _log2(cblk))


def _tile_rows(x, n):
    return x if n == 1 else jnp.concatenate([x] * n, axis=0)


class _BlockDiag:
    def __init__(self, c, n, cb, ref=None):
        self.c, self.n, self.cb, self.ref = c, n, cb, ref
        self.slot = 0
        self.mask = None if ref is not None else _block_mask(n * c, n * cb, c, cb)

    def __call__(self, x):
        c, n, cb = self.c, self.n, self.cb
        if self.ref is None:
            return jnp.where(self.mask, _tile_rows(x.astype(F32), n), 0.0).astype(BF16)
        slot, self.slot = self.slot, self.slot + 1
        xb = x.astype(BF16)
        for h in range(n):
            self.ref[slot, h * c:(h + 1) * c, h * cb:(h + 1) * cb] = xb[:, h * cb:(h + 1) * cb]
        return self.ref[slot]


def _shared_rhs_dots(xs, rhs):
    m = xs[0].shape[0]
    if len(xs) == 1 or m % BF16_ROWS != 0:
        return [jnp.dot(x.astype(BF16), rhs, preferred_element_type=F32) for x in xs]
    t = jnp.dot(jnp.concatenate([x.astype(BF16) for x in xs], axis=0), rhs, preferred_element_type=F32)
    return [t[i * m:(i + 1) * m] for i in range(len(xs))]


_INVERSE_SLOTS = lambda c: 3 + (1 + _log2(c // SUBLANES) if c > SUBLANES else 0)


def _unit_lower_inverse_wide(mats, c, n, bd):
    ri = lax.broadcasted_iota(jnp.int32, (c, n * c), 0)
    cj = lax.broadcasted_iota(jnp.int32, (c, n * c), 1) & (c - 1)
    eye = (ri == cj).astype(F32)
    base = (ri >> 3) == (cj >> 3)
    idx = range(len(mats))
    nm = [jnp.where(base, -a, 0.0) for a in mats]
    p = [eye + x for x in nm]
    n2 = [_shared_rhs_dots([nm[i]], bd(nm[i]))[0] for i in idx]
    pn = [_shared_rhs_dots([p[i], n2[i]], bd(n2[i])) for i in idx]
    p = [p[i] + pn[i][0] for i in idx]
    p = [p[i] + _shared_rhs_dots([p[i]], bd(pn[i][1]))[0] for i in idx]
    if c > SUBLANES:
        a_bd = [bd(a) for a in mats]
    s = SUBLANES
    while s < c:
        pa = [_shared_rhs_dots([p[i]], a_bd[i])[0] for i in idx]
        e = [_shared_rhs_dots([pa[i]], bd(p[i]))[0] for i in idx]
        shift = _log2(2 * s)
        merge = ((ri >> shift) == (cj >> shift)) & ((ri & s) != 0) & ((cj & s) == 0)
        p = [p[i] - jnp.where(merge, e[i], 0.0) for i in idx]
        s *= 2
    return p


def _block_diag_scratch(chunk, nb, heads, head_dim, *, channel_sites):
    if chunk % BF16_ROWS != 0:
        return []
    hg = min(heads, MXU_TILE // chunk)
    ntg = nb * (heads // hg)
    return [pltpu.VMEM((channel_sites * ntg, hg * chunk, hg * head_dim), BF16),
            pltpu.VMEM((_INVERSE_SLOTS(chunk) * ntg, hg * chunk, hg * chunk), BF16)]


def _ffn_body(h_ref, gin_ref, gout_ref, gnext_ref, wg_ref, wu_ref, wo_ref, *rest, emit_u):
    if emit_u:
        hout_ref, u_ref, xn_sc, acc_sc = rest
    else:
        hout_ref, xn_sc, acc_sc = rest
    j = pl.program_id(1)

    @pl.when(j == 0)
    def _():
        xn_sc[...] = _rms(h_ref[...], gin_ref[...]).astype(BF16)
        acc_sc[...] = jnp.zeros_like(acc_sc)

    xn = xn_sc[...]
    gate = jnp.dot(xn, wg_ref[...], preferred_element_type=F32)
    up = jnp.dot(xn, wu_ref[...], preferred_element_type=F32)
    act = (gate * jax.nn.sigmoid(gate)) * up
    acc_sc[...] += jnp.dot(act.astype(BF16), wo_ref[...], preferred_element_type=F32)

    @pl.when(j == pl.num_programs(1) - 1)
    def _():
        hn = h_ref[...] + 0.5 * _rms(acc_sc[...], gout_ref[...])
        hout_ref[...] = hn
        if emit_u:
            u_ref[...] = _rms(hn, gnext_ref[...])


def _ffn(h, g_in, g_out, g_next, w_in, w_out, layer, half, *, emit_u, tm, tf):
    n, d = h.shape
    f = w_out.shape[2]
    nf = f // tf
    row = pl.BlockSpec((tm, d), lambda i, j: (i, 0))
    vec = pl.BlockSpec((1, d), lambda i, j: (0, 0))
    out_shape = [jax.ShapeDtypeStruct((n, d), F32)]
    out_specs = [row]
    if emit_u:
        out_shape.append(jax.ShapeDtypeStruct((n, d), F32))
        out_specs.append(row)
    return pl.pallas_call(
        functools.partial(_ffn_body, emit_u=emit_u),
        grid=(n // tm, nf),
        in_specs=[row, vec, vec, vec,
                  pl.BlockSpec((None, None, d, tf), lambda i, j: (layer, half, 0, j)),
                  pl.BlockSpec((None, None, d, tf), lambda i, j: (layer, half, 0, j + nf)),
                  pl.BlockSpec((None, None, tf, d), lambda i, j: (layer, half, j, 0))],
        out_specs=out_specs,
        out_shape=out_shape,
        scratch_shapes=[pltpu.VMEM((tm, d), BF16), pltpu.VMEM((tm, d), F32)],
        compiler_params=_params("parallel", "arbitrary"),
        name="ffn_u" if emit_u else "ffn",
    )(h, g_in, g_out, g_next, w_in, w_in, w_out)


def _gdn_proj_body(u_ref, w_ref, wba_ref, main_ref, ba_ref, xb_sc):
    j = pl.program_id(1)

    @pl.when(j == 0)
    def _():
        xb = u_ref[...].astype(BF16)
        xb_sc[...] = xb
        ba_ref[...] = jnp.dot(xb, wba_ref[...], preferred_element_type=F32)

    main_ref[...] = jnp.dot(xb_sc[...], w_ref[...], preferred_element_type=F32)


def _gdn_proj(u, w_main, w_ba, *, tm, tn):
    n, d = u.shape
    nm = w_main.shape[1]
    return pl.pallas_call(
        _gdn_proj_body,
        grid=(n // tm, nm // tn),
        in_specs=[pl.BlockSpec((tm, d), lambda i, j: (i, 0)),
                  pl.BlockSpec((d, tn), lambda i, j: (0, j)),
                  pl.BlockSpec((d, LANES), lambda i, j: (0, 0))],
        out_specs=[pl.BlockSpec((tm, tn), lambda i, j: (i, j)),
                   pl.BlockSpec((tm, LANES), lambda i, j: (i, 0))],
        out_shape=[jax.ShapeDtypeStruct((n, nm), F32), jax.ShapeDtypeStruct((n, LANES), F32)],
        scratch_shapes=[pltpu.VMEM((tm, d), BF16)],
        compiler_params=_params("parallel", "arbitrary"),
        name="gdn_proj",
    )(u, w_main, w_ba)


def _gdn_rec_body(cur_ref, prev_ref, init_ref, ba_ref, cw_ref, alr_ref, dbr_ref,
                  s0_ref, sacc_ref, o_ref, sout_ref, ext_sc, s_sc, *bd_scratch, chunk, t_valid, nb):
    del sacc_ref
    c = pl.program_id(1)
    hd = GDN_HEAD_DIM
    nh = GDN_HEADS
    sgh = MXU_TILE // hd
    nsg = nh // sgh

    @pl.when(c == 0)
    def _():
        s_sc[...] = jnp.zeros_like(s_sc)
        for q in range(nb):
            for h in range(nh):
                g, j = divmod(h, sgh)
                s_sc[q * nsg + g, j * hd:(j + 1) * hd, j * hd:(j + 1) * hd] = s0_ref[q, h]
        for ref in bd_scratch:
            ref[...] = jnp.zeros_like(ref)
        ext_sc[:, 0:SUBLANES, :] = init_ref[...]

    @pl.when(c > 0)
    def _():
        ext_sc[:, 0:SUBLANES, :] = prev_ref[...]

    ext_sc[:, SUBLANES:SUBLANES + chunk, :] = cur_ref[...]
    incl, _ = _tri_incl(chunk)
    tri_bf = incl.astype(BF16)

    kb, qn, kn, vb, kbeg, qdec, kdec, gcol, dlast = ([] for _ in range(9))
    for s_ in range(nb):
        first = SUBLANES - (GDN_CONV - 1)
        conv = cw_ref[0:1, :] * ext_sc[s_, first:first + chunk, :]
        for tap in range(1, GDN_CONV):
            conv = conv + cw_ref[tap:tap + 1, :] * ext_sc[s_, first + tap:first + tap + chunk, :]
        qkv = conv * jax.nn.sigmoid(conv)

        ba = ba_ref[s_]
        g_cols = -jnp.exp(alr_ref[...]) * _softplus(ba + dbr_ref[...])
        beta_cols = jax.nn.sigmoid(ba)
        if t_valid < chunk:
            row_ok = lax.broadcasted_iota(jnp.int32, (chunk, 1), 0) < t_valid
            qkv = jnp.where(row_ok, qkv, 0.0)
            g_cols = jnp.where(row_ok, g_cols, 0.0)
        gc_cols = _cumsum_rows(tri_bf, g_cols)

        for h in range(nh):
            q = qkv[:, h * hd:(h + 1) * hd]
            k = qkv[:, (nh + h) * hd:(nh + h + 1) * hd]
            v = qkv[:, (2 * nh + h) * hd:(2 * nh + h + 1) * hd]
            qn_h = q * lax.rsqrt(jnp.sum(q * q, axis=-1, keepdims=True) + L2_EPS) * (hd ** -0.5)
            kn_h = k * lax.rsqrt(jnp.sum(k * k, axis=-1, keepdims=True) + L2_EPS)
            beta = beta_cols[:, h:h + 1]
            gc = gc_cols[:, nh + h:nh + h + 1]
            eg = jnp.exp(gc)
            g_last = gc[chunk - 1:chunk, :]
            kb_h = kn_h * beta
            kb.append(kb_h)
            qn.append(qn_h)
            kn.append(kn_h)
            vb.append(v * beta)
            kbeg.append(kb_h * eg)
            qdec.append(qn_h * eg)
            kdec.append(kn_h * jnp.exp(g_last - gc))
            gcol.append(gc)
            dlast.append(jnp.exp(g_last))

    hg = min(nh, MXU_TILE // chunk)
    ntg = nh // hg
    wt = hg * chunk
    kt = hg * hd
    sg_per_tg = kt // MXU_TILE
    tgs = range(nb * ntg)
    cat = lambda xs, t: jnp.concatenate(xs[t * hg:(t + 1) * hg], axis=1)
    bdc_ref, bdt_ref = bd_scratch if bd_scratch else (None, None)
    bd_c = _BlockDiag(chunk, hg, hd, bdc_ref)
    bd_t = _BlockDiag(chunk, hg, chunk, bdt_ref)
    state_mask = _block_mask(MXU_TILE, MXU_TILE, hd, hd)
    ri = lax.broadcasted_iota(jnp.int32, (chunk, wt), 0)
    cw = lax.broadcasted_iota(jnp.int32, (chunk, wt), 1)
    cj = cw & (chunk - 1)
    strict_w = ri > cj
    incl_w = ri >= cj
    ones_cc = jnp.ones((chunk, chunk), BF16)

    decay = []
    for t in tgs:
        gce = jnp.broadcast_to(gcol[t * hg], (chunk, wt))
        for j in range(1, hg):
            gce = jnp.where(cw >= j * chunk, jnp.broadcast_to(gcol[t * hg + j], (chunk, wt)), gce)
        hi, mid, lo = _split3(jnp.where(ri == cj, gce, 0.0))
        gcj = (jnp.dot(ones_cc, hi, preferred_element_type=F32)
               + (jnp.dot(ones_cc, mid, preferred_element_type=F32) + jnp.dot(ones_cc, lo, preferred_element_type=F32)))
        decay.append(jnp.exp(jnp.where(incl_w, gce - gcj, -jnp.inf)))

    lhs = [jnp.concatenate([cat(kb, t), cat(qn, t)], axis=0).astype(BF16) for t in tgs]
    mk = [lax.dot_general(lhs[t], bd_c(cat(kn, t)), (((1,), (1,)), ((), ())),
                          preferred_element_type=F32) for t in tgs]
    a_mat = [jnp.where(strict_w, mk[t][:chunk] * decay[t], 0.0) for t in tgs]
    attn = [jnp.where(incl_w, mk[t][chunk:] * decay[t], 0.0) for t in tgs]
    t_mat = _unit_lower_inverse_wide(a_mat, chunk, hg, bd_t)
    t_bf = [x.astype(BF16) for x in t_mat]
    u = [jnp.dot(t_bf[t], bd_c(cat(vb, t)), preferred_element_type=F32) for t in tgs]
    w = [jnp.dot(t_bf[t], bd_c(cat(kbeg, t)), preferred_element_type=F32) for t in tgs]

    v_new, qs = [], []
    for t in tgs:
        qd = cat(qdec, t)
        parts = []
        for q_ in range(sg_per_tg):
            sl = slice(q_ * MXU_TILE, (q_ + 1) * MXU_TILE)
            parts.append(_dot(jnp.concatenate([w[t][:, sl], qd[:, sl]], axis=0), s_sc[t * sg_per_tg + q_]))
        ws = parts[0] if sg_per_tg == 1 else jnp.concatenate(parts, axis=1)
        v_new.append(u[t] - ws[:chunk])
        qs.append(ws[chunk:])
    o = [qs[t] + jnp.dot(attn[t].astype(BF16), bd_c(v_new[t]), preferred_element_type=F32)
         for t in tgs]
    join = lambda xs, s_: xs[s_] if ntg == 1 else jnp.concatenate(xs[s_ * ntg:(s_ + 1) * ntg], axis=1)
    lane = lax.broadcasted_iota(jnp.int32, (1, MXU_TILE), 1)
    for s_ in range(nb):
        o_ref[s_] = join(o, s_)
        vn_all = join(v_new, s_)
        for g in range(nsg):
            sl = slice(g * MXU_TILE, (g + 1) * MXU_TILE)
            h0 = s_ * nh + g * sgh
            dl = jnp.broadcast_to(dlast[h0], (1, MXU_TILE))
            for j in range(1, sgh):
                dl = jnp.where(lane >= j * hd, jnp.broadcast_to(dlast[h0 + j], (1, MXU_TILE)), dl)
            upd = _dot_tn(jnp.concatenate(kdec[h0:h0 + sgh], axis=1), vn_all[:, sl])
            s_sc[s_ * nsg + g] = s_sc[s_ * nsg + g] * dl + jnp.where(state_mask, upd, 0.0)

    @pl.when(c == pl.num_programs(1) - 1)
    def _():
        for s_ in range(nb):
            for h in range(nh):
                g, j = divmod(h, sgh)
                sout_ref[s_, h] = s_sc[s_ * nsg + g, j * hd:(j + 1) * hd, j * hd:(j + 1) * hd]


def _gdn_rec(qkv, conv_init, ba, conv_w, alr, dbr, s0, s_acc, layer, *, chunk, t_valid, nb):
    bsz, tp, width = qkv.shape
    assert bsz % nb == 0 and tp % chunk == 0
    state = pl.BlockSpec((None, nb, GDN_HEADS, GDN_HEAD_DIM, GDN_HEAD_DIM), lambda b, c: (layer, b, 0, 0, 0))
    nc = tp // chunk
    cb = chunk // SUBLANES
    dv = GDN_HEADS * GDN_HEAD_DIM
    full2 = lambda shape: pl.BlockSpec(shape, lambda b, c: (0, 0))
    return pl.pallas_call(
        functools.partial(_gdn_rec_body, chunk=chunk, t_valid=t_valid, nb=nb),
        grid=(bsz // nb, nc),
        in_specs=[pl.BlockSpec((nb, chunk, GDN_QKV), lambda b, c: (b, c, 0)),
                  pl.BlockSpec((nb, SUBLANES, GDN_QKV), lambda b, c: (b, jnp.maximum(c * cb - 1, 0), 0)),
                  pl.BlockSpec((nb, SUBLANES, GDN_QKV), lambda b, c: (b, 0, 0)),
                  pl.BlockSpec((nb, chunk, LANES), lambda b, c: (b, c, 0)),
                  full2(conv_w.shape), full2(alr.shape), full2(dbr.shape),
                  state, pl.BlockSpec(memory_space=pl.ANY)],
        out_specs=[pl.BlockSpec((nb, chunk, dv), lambda b, c: (b, c, 0)), state],
        out_shape=[jax.ShapeDtypeStruct((bsz, tp, dv), F32), jax.ShapeDtypeStruct(s_acc.shape, F32)],
        input_output_aliases={8: 1},
        scratch_shapes=[pltpu.VMEM((nb, chunk + SUBLANES, GDN_QKV), F32),
                        pltpu.VMEM((nb * dv // MXU_TILE, MXU_TILE, MXU_TILE), F32)]
        + _block_diag_scratch(chunk, nb, GDN_HEADS, GDN_HEAD_DIM, channel_sites=4),
        compiler_params=_params("parallel", "arbitrary"),
        name="gdn_rec",
    )(qkv, qkv, conv_init, ba, conv_w, alr, dbr, s0, s_acc)


def _gdn_post_body(o_ref, z_ref, h_ref, on_ref, w_ref, g_ref, hout_ref, gated_sc):
    hd = GDN_HEAD_DIM
    for h in range(GDN_HEADS):
        sl = slice(h * hd, (h + 1) * hd)
        o = o_ref[:, sl]
        z = z_ref[:, sl]
        on = o * lax.rsqrt(jnp.mean(o * o, axis=-1, keepdims=True) + NORM_EPS) * on_ref[...]
        gated_sc[:, sl] = (on * (z * jax.nn.sigmoid(z))).astype(BF16)
    y = jnp.dot(gated_sc[...], w_ref[...], preferred_element_type=F32)
    hout_ref[...] = h_ref[...] + _rms(y, g_ref[...])


def _gdn_post(o, main, h, o_norm, w_out, g, *, tm):
    n, d = h.shape
    dv = o.shape[1]
    zblk = GDN_QKV // dv
    row = pl.BlockSpec((tm, d), lambda i: (i, 0))
    return pl.pallas_call(
        _gdn_post_body,
        grid=(n // tm,),
        in_specs=[pl.BlockSpec((tm, dv), lambda i: (i, 0)),
                  pl.BlockSpec((tm, dv), lambda i: (i, zblk)),
                  row,
                  pl.BlockSpec((1, GDN_HEAD_DIM), lambda i: (0, 0)),
                  pl.BlockSpec((dv, d), lambda i: (0, 0)),
                  pl.BlockSpec((1, d), lambda i: (0, 0))],
        out_specs=row,
        out_shape=jax.ShapeDtypeStruct((n, d), F32),
        scratch_shapes=[pltpu.VMEM((tm, dv), BF16)],
        compiler_params=_params("parallel"),
        name="gdn_post",
    )(o, main, h, o_norm, w_out, g)


def _rwkv_proj_body(*refs, has_vres, seq_len, tm):
    if has_vres:
        (x_ref, prev_ref, start_ref, vf_ref, lerp_ref, wr_ref, wk_ref, wv_ref, w0_ref, w1_ref, w2_ref, a0_ref,
         a1_ref, a2_ref, g1_ref, g2_ref, v0_ref, v1_ref, v2_ref, r_ref, k_ref, v_ref, lw_ref, a_ref, g_ref) = refs
    else:
        (x_ref, prev_ref, start_ref, lerp_ref, wr_ref, wk_ref, wv_ref, w0_ref, w1_ref, w2_ref, a0_ref, a1_ref,
         a2_ref, g1_ref, g2_ref, r_ref, k_ref, v_ref, lw_ref, a_ref, g_ref) = refs
    x = x_ref[...]
    rolled = pltpu.roll(x, 1, axis=0)
    rows = lax.broadcasted_iota(jnp.int32, (tm, 1), 0)
    if tm <= seq_len:
        i = pl.program_id(0)
        tiles_per_seq = seq_len // tm
        carried = start_ref[pl.ds(i // tiles_per_seq, 1), :]
        before = prev_ref[SUBLANES - 1:SUBLANES, :]
        first = jnp.where(i % tiles_per_seq == 0, carried, before)
        xp = jnp.where(rows == 0, first, rolled)
    else:
        xp = jnp.where((rows & (seq_len - 1)) == 0, start_ref[...], rolled)
    dx = xp - x

    def mix(s):
        return (x + dx * lerp_ref[s:s + 1, :]).astype(BF16)

    r_ref[...] = jnp.dot(mix(0), wr_ref[...], preferred_element_type=F32)
    k_ref[...] = jnp.dot(mix(1), wk_ref[...], preferred_element_type=F32)
    xv = mix(2)
    v = jnp.dot(xv, wv_ref[...], preferred_element_type=F32)
    if has_vres:
        gate = jax.nn.sigmoid(v0_ref[...] + _dot(jnp.dot(xv, v1_ref[...], preferred_element_type=F32), v2_ref[...]))
        v = v + (vf_ref[...] - v) * gate
    v_ref[...] = v
    dec = w0_ref[...] + _dot(jnp.tanh(jnp.dot(mix(3), w1_ref[...], preferred_element_type=F32)), w2_ref[...])
    logw = -_softplus(-dec) - 0.5
    lw_ref[...] = -jnp.exp(logw)
    a_ref[...] = jax.nn.sigmoid(a0_ref[...] + _dot(jnp.dot(mix(4), a1_ref[...], preferred_element_type=F32),
                                                   a2_ref[...]))
    g_ref[...] = _dot(jax.nn.sigmoid(jnp.dot(mix(5), g1_ref[...], preferred_element_type=F32)), g2_ref[...])


def _rwkv_proj(x, shift, v_first, lerp, wr, wk, wv, w0, w1, w2, a0, a1, a2, g1, g2, vres, *, seq_len, tm):
    n, d = x.shape
    has_vres = vres is not None
    row = pl.BlockSpec((tm, d), lambda i: (i, 0))
    full = lambda a: pl.BlockSpec(a.shape, lambda i: (0, 0))
    tb = tm // SUBLANES
    if tm <= seq_len:
        assert seq_len % tm == 0
        start, start_spec = shift, full(shift)
    else:
        assert tm % seq_len == 0 and seq_len & (seq_len - 1) == 0
        start, start_spec = jnp.repeat(shift, seq_len, axis=0), row
    prev_spec = pl.BlockSpec((SUBLANES, d), lambda i: (jnp.maximum(i * tb - 1, 0), 0))
    args = [x, x, start] + ([v_first] if has_vres else []) + [lerp, wr, wk, wv, w0, w1, w2, a0, a1, a2, g1, g2]
    n_rows = 4 if has_vres else 3
    in_specs = [row, prev_spec, start_spec] + ([row] if has_vres else []) + [full(a) for a in args[n_rows:]]
    if has_vres:
        args += list(vres)
        in_specs += [full(a) for a in vres]
    return pl.pallas_call(
        functools.partial(_rwkv_proj_body, has_vres=has_vres, seq_len=seq_len, tm=tm),
        grid=(n // tm,),
        in_specs=in_specs,
        out_specs=[row] * 6,
        out_shape=[jax.ShapeDtypeStruct((n, d), F32)] * 6,
        compiler_params=_params("parallel"),
        name="rwkv_proj_vres" if has_vres else "rwkv_proj",
    )(*args)


def _rwkv_rec_body(r_ref, k_ref, v_ref, lw_ref, a_ref, kk_ref, ka_ref, rk_ref, lnw_ref, lnb_ref, s0_ref,
                   sacc_ref, y_ref, sout_ref, s_sc, *bd_scratch, chunk, t_valid, heads, nb):
    del sacc_ref
    c = pl.program_id(1)
    hd = RWKV_HEAD
    d = heads * hd
    gw = MXU_TILE
    sgh = gw // hd
    nsg = heads // sgh
    hg = min(heads, gw // chunk)
    ntg = heads // hg
    kt = hg * hd
    sg_per_tg = kt // gw

    @pl.when(c == 0)
    def _():
        s_sc[...] = jnp.zeros_like(s_sc)
        for q in range(nb):
            for h in range(heads):
                g, j = divmod(h, sgh)
                s_sc[q * nsg + g, j * hd:(j + 1) * hd, j * hd:(j + 1) * hd] = s0_ref[q, h]
        for ref in bd_scratch:
            ref[...] = jnp.zeros_like(ref)

    state_mask = _block_mask(gw, gw, hd, hd)
    ones_bd = state_mask.astype(BF16)

    def head_sums(xs):
        outs = [[] for _ in xs]
        for g in range(nsg):
            parts = []
            for x in xs:
                parts.extend(_split2(x[:, g * gw:(g + 1) * gw]))
            sums = _shared_rhs_dots(parts, ones_bd)
            for i in range(len(xs)):
                outs[i].append(sums[2 * i] + sums[2 * i + 1])
        return [jnp.concatenate(o, axis=1) for o in outs]

    incl, _ = _tri_incl(chunk)
    tri_bf = incl.astype(BF16)

    def prep(q):
        r_all = r_ref[q]
        k_all = k_ref[q]
        v_all = v_ref[q]
        lw_all = lw_ref[q]
        a_all = a_ref[q]
        if t_valid < chunk:
            row_ok = lax.broadcasted_iota(jnp.int32, (chunk, 1), 0) < t_valid
            r_all = jnp.where(row_ok, r_all, 0.0)
            k_all = jnp.where(row_ok, k_all, 0.0)
            v_all = jnp.where(row_ok, v_all, 0.0)
            lw_all = jnp.where(row_ok, lw_all, 0.0)
        gc = _cumsum_rows(tri_bf, lw_all)
        kx = k_all * kk_ref[...]
        kmod = k_all * (1.0 + (a_all - 1.0) * ka_ref[...])
        kx_ss, rk_sum = head_sums([kx * kx, r_all * kmod * rk_ref[...]])
        kk = kx * lax.rsqrt(kx_ss + L2_EPS)
        bv = kk * a_all
        e_neg = jnp.exp(-gc)
        g_last = gc[chunk - 1:chunk, :]
        e_rem = jnp.exp(g_last - gc)
        return dict(v=v_all, rk_sum=rk_sum, at=-kk * jnp.exp(gc - lw_all), rt=r_all * jnp.exp(gc),
                    bt=bv * e_neg, kt=kmod * e_neg, bp=bv * e_rem, kp=kmod * e_rem, dec=jnp.exp(g_last))

    seqs = [prep(q) for q in range(nb)]

    bdc_ref, bdt_ref = bd_scratch if bd_scratch else (None, None)
    bd_c = _BlockDiag(chunk, hg, hd, bdc_ref)
    bd_t = _BlockDiag(chunk, hg, chunk, bdt_ref)
    ri = lax.broadcasted_iota(jnp.int32, (chunk, hg * chunk), 0)
    cj = lax.broadcasted_iota(jnp.int32, (chunk, hg * chunk), 1) & (chunk - 1)
    strict_w = ri > cj
    incl_w = ri >= cj

    tgs = range(nb * ntg)
    lane = lambda name, t: seqs[t // ntg][name][:, (t % ntg) * kt:(t % ntg + 1) * kt]
    lhs = [jnp.concatenate([lane("at", t), lane("rt", t)], axis=0).astype(BF16) for t in tgs]
    mb = [lax.dot_general(lhs[t], bd_c(lane("bt", t)), (((1,), (1,)), ((), ())),
                          preferred_element_type=F32) for t in tgs]
    mk = [lax.dot_general(lhs[t], bd_c(lane("kt", t)), (((1,), (1,)), ((), ())),
                          preferred_element_type=F32) for t in tgs]
    a_ab = [jnp.where(strict_w, mb[t][:chunk], 0.0) for t in tgs]
    a_rb = [jnp.where(incl_w, mb[t][chunk:], 0.0) for t in tgs]
    a_ak = [jnp.where(strict_w, mk[t][:chunk], 0.0) for t in tgs]
    a_rk = [jnp.where(incl_w, mk[t][chunk:], 0.0) for t in tgs]
    t_mat = _unit_lower_inverse_wide([-x for x in a_ab], chunk, hg, bd_t)

    xs = []
    for t in tgs:
        parts = []
        for q in range(sg_per_tg):
            g = t * sg_per_tg + q
            parts.append(lax.dot_general(lhs[t][:, q * gw:(q + 1) * gw], s_sc[g].astype(BF16),
                                         (((1,), (1,)), ((), ())), preferred_element_type=F32))
        xs.append(parts[0] if len(parts) == 1 else jnp.concatenate(parts, axis=1))
    av = [_shared_rhs_dots([a_ak[t], a_rk[t]], bd_c(lane("v", t))) for t in tgs]
    rhs_u = [xs[t][:chunk] + av[t][0] for t in tgs]
    u = [jnp.dot(t_mat[t].astype(BF16), bd_c(rhs_u[t]), preferred_element_type=F32) for t in tgs]
    y = [xs[t][chunk:] + jnp.dot(a_rb[t].astype(BF16), bd_c(u[t]), preferred_element_type=F32) + av[t][1]
         for t in tgs]

    join = lambda xs_, q: xs_[q] if ntg == 1 else jnp.concatenate(xs_[q * ntg:(q + 1) * ntg], axis=1)
    for q in range(nb):
        sq = seqs[q]
        u_all = join(u, q)
        for g in range(nsg):
            sl = slice(g * gw, (g + 1) * gw)
            upd = _dot_tn(jnp.concatenate([u_all[:, sl], sq["v"][:, sl]], axis=0),
                          jnp.concatenate([sq["bp"][:, sl], sq["kp"][:, sl]], axis=0))
            s_sc[q * nsg + g] = s_sc[q * nsg + g] * sq["dec"][:, sl] + jnp.where(state_mask, upd, 0.0)

    y_all = [join(y, q) for q in range(nb)]
    yc = [y_all[q] - head_sums([y_all[q]])[0] * (1.0 / hd) for q in range(nb)]
    var = [head_sums([yc[q] * yc[q]])[0] * (1.0 / hd) for q in range(nb)]
    for q in range(nb):
        yn = yc[q] * lax.rsqrt(var[q] + GN_EPS)
        y_ref[q] = yn * lnw_ref[...] + lnb_ref[...] + seqs[q]["rk_sum"] * seqs[q]["v"]

    @pl.when(c == pl.num_programs(1) - 1)
    def _():
        for q in range(nb):
            for h in range(heads):
                g, j = divmod(h, sgh)
                sout_ref[q, h] = s_sc[q * nsg + g, j * hd:(j + 1) * hd, j * hd:(j + 1) * hd]


def _rwkv_rec(r, k, v, lw, a, k_k, k_a, r_k, ln_w, ln_b, s0, s_acc, layer, *, chunk, t_valid, nb):
    bsz, tp, d = r.shape
    assert bsz % nb == 0 and tp % chunk == 0
    heads = d // RWKV_HEAD
    nc = tp // chunk
    seq = pl.BlockSpec((nb, chunk, d), lambda b, c: (b, c, 0))
    vec = pl.BlockSpec((1, d), lambda b, c: (0, 0))
    st = pl.BlockSpec((None, nb, heads, RWKV_HEAD, RWKV_HEAD), lambda b, c: (layer, b, 0, 0, 0))
    return pl.pallas_call(
        functools.partial(_rwkv_rec_body, chunk=chunk, t_valid=t_valid, heads=heads, nb=nb),
        grid=(bsz // nb, nc),
        in_specs=[seq] * 5 + [vec] * 5 + [st, pl.BlockSpec(memory_space=pl.ANY)],
        out_specs=[seq, st],
        out_shape=[jax.ShapeDtypeStruct((bsz, tp, d), F32), jax.ShapeDtypeStruct(s_acc.shape, F32)],
        input_output_aliases={11: 1},
        scratch_shapes=[pltpu.VMEM((nb * d // MXU_TILE, MXU_TILE, MXU_TILE), F32)]
        + _block_diag_scratch(chunk, nb, heads, RWKV_HEAD, channel_sites=5),
        compiler_params=_params("parallel", "arbitrary"),
        name="rwkv_rec",
    )(r, k, v, lw, a, k_k, k_a, r_k, ln_w, ln_b, s0, s_acc)


def _rwkv_post_body(y_ref, gate_ref, h_ref, w_ref, g_ref, hout_ref):
    out = jnp.dot((y_ref[...] * gate_ref[...]).astype(BF16), w_ref[...], preferred_element_type=F32)
    hout_ref[...] = h_ref[...] + _rms(out, g_ref[...])


def _rwkv_post(y, gate, h, w_o, g, *, tm):
    n, d = h.shape
    row = pl.BlockSpec((tm, d), lambda i: (i, 0))
    return pl.pallas_call(
        _rwkv_post_body,
        grid=(n // tm,),
        in_specs=[row, row, row, pl.BlockSpec((d, d), lambda i: (0, 0)), pl.BlockSpec((1, d), lambda i: (0, 0))],
        out_specs=row,
        out_shape=jax.ShapeDtypeStruct((n, d), F32),
        compiler_params=_params("parallel"),
        name="rwkv_post",
    )(y, gate, h, w_o, g)


def _pad_time(x, tp):
    t = x.shape[1]
    if t == tp:
        return x
    return jnp.pad(x, [(0, 0), (0, tp - t)] + [(0, 0)] * (x.ndim - 2))


def _tiles(n):
    return 1024 if n % 1024 == 0 else n


def _trunk(h, bsz, t, gdn_s, gdn_buf, rw_s, rw_shift, p, chunk, nb):
    n, d = h.shape
    depth = p["norm_g"].shape[0]
    tm = _tiles(n)
    tp = -(-t // chunk) * chunk
    out_gdn_buf, out_rw_shift = [], []
    out_gdn_s = jnp.zeros_like(gdn_s)
    out_rw_s = jnp.zeros_like(rw_s)
    v_first = None
    for i in range(depth):
        ng = p["norm_g"][i]
        row = lambda s: ng[s][None, :]
        h, u = _ffn(h, row(0), row(1), row(2), p["ffn_w_in"], p["ffn_w_out"], i, 0,
                    emit_u=True, tm=tm, tf=256)
        j = i // 2
        if i % 2 == 0:
            main, ba = _gdn_proj(u, p["gdn_w_main"][j], p["gdn_w_ba"][j], tm=tm, tn=1024)
            main3 = main.reshape(bsz, t, main.shape[1])
            out_gdn_buf.append(main3[:, t - (GDN_CONV - 1):, :GDN_QKV])
            ba3 = _pad_time(ba.reshape(bsz, t, LANES), tp)
            conv_init = jnp.pad(gdn_buf[j], [(0, 0), (SUBLANES - (GDN_CONV - 1), 0), (0, 0)])
            qkv = main3 if tp == t else _pad_time(main3[:, :, :GDN_QKV], tp)
            o, out_gdn_s = _gdn_rec(qkv, conv_init, ba3, p["gdn_conv_w"][j], p["gdn_alr"][j], p["gdn_dbr"][j],
                                    gdn_s, out_gdn_s, j, chunk=chunk, t_valid=min(t, chunk), nb=nb)
            o2 = o[:, :t].reshape(n, o.shape[2])
            h = _gdn_post(o2, main, h, p["gdn_o_norm"][j], p["gdn_w_out"][j], row(3), tm=min(tm, 512))
        else:
            u3 = u.reshape(bsz, t, d)
            out_rw_shift.append(u3[:, -1])
            vres = None if j == 0 else (p["rwkv_v0"][j - 1], p["rwkv_v1"][j - 1], p["rwkv_v2"][j - 1])
            r, k, v, lw, a, g = _rwkv_proj(
                u, rw_shift[j], v_first, p["rwkv_lerp"][j], p["rwkv_w_r"][j], p["rwkv_w_k"][j], p["rwkv_w_v"][j],
                p["rwkv_w0"][j], p["rwkv_w1"][j], p["rwkv_w2"][j], p["rwkv_a0"][j], p["rwkv_a1"][j],
                p["rwkv_a2"][j], p["rwkv_g1"][j], p["rwkv_g2"][j], vres, seq_len=t, tm=min(tm, 256))
            if v_first is None:
                v_first = v
            seq = lambda x: _pad_time(x.reshape(bsz, t, d), tp)
            y, out_rw_s = _rwkv_rec(seq(r), seq(k), seq(v), seq(lw), seq(a), p["rwkv_k_k"][j], p["rwkv_k_a"][j],
                                    p["rwkv_r_k"][j], p["rwkv_ln_w"][j], p["rwkv_ln_b"][j], rw_s, out_rw_s, j,
                                    chunk=chunk, t_valid=min(t, chunk), nb=nb)
            h = _rwkv_post(y[:, :t].reshape(n, d), g, h, p["rwkv_w_o"][j], row(3), tm=min(tm, 512))
        h = _ffn(h, row(4), row(5), row(5), p["ffn_w_in"], p["ffn_w_out"], i, 1,
                 emit_u=False, tm=tm, tf=256)[0]
    return h.reshape(bsz, t, d), out_gdn_s, jnp.stack(out_gdn_buf), out_rw_s, jnp.stack(out_rw_shift)


def kernel(x_prompt, x_sample, state_gdn, state_gdn_conv, state_rwkv, state_rwkv_shift, norm_g, ffn_w_in, ffn_w_out, gdn_w_in, gdn_conv_w, gdn_a_log, gdn_dt_bias, gdn_o_norm, gdn_w_out, rwkv_lerp, rwkv_w_rkv, rwkv_w0, rwkv_w1, rwkv_w2, rwkv_a0, rwkv_a1, rwkv_a2, rwkv_v0, rwkv_v1, rwkv_v2, rwkv_g1, rwkv_g2, rwkv_k_k, rwkv_k_a, rwkv_r_k, rwkv_ln_w, rwkv_ln_b, rwkv_w_o):
    bsz, t, d = x_prompt.shape
    dbsz, dt, _ = x_sample.shape
    n_gdn = gdn_w_in.shape[0]
    n_rwkv = rwkv_w_rkv.shape[0]
    nh = GDN_HEADS
    n_main = GDN_QKV + nh * GDN_HEAD_DIM

    def lane_slot(x, offset):
        return jnp.pad(x, [(0, 0), (offset, LANES - offset - nh)])[:, None, :]

    row3 = lambda x: x[:, None, :]
    p = dict(
        norm_g=norm_g,
        ffn_w_in=ffn_w_in.astype(BF16),
        ffn_w_out=ffn_w_out.astype(BF16),
        gdn_w_main=gdn_w_in[:, :, :n_main].astype(BF16),
        gdn_w_ba=jnp.pad(gdn_w_in[:, :, n_main:], [(0, 0), (0, 0), (0, LANES - 2 * nh)]).astype(BF16),
        gdn_conv_w=gdn_conv_w,
        gdn_alr=lane_slot(gdn_a_log, nh), gdn_dbr=lane_slot(gdn_dt_bias, nh),
        gdn_o_norm=row3(gdn_o_norm),
        gdn_w_out=gdn_w_out.astype(BF16),
        rwkv_lerp=jnp.pad(rwkv_lerp, [(0, 0), (0, SUBLANES - rwkv_lerp.shape[1]), (0, 0)]),
        rwkv_w_r=rwkv_w_rkv[:, 0].astype(BF16), rwkv_w_k=rwkv_w_rkv[:, 1].astype(BF16),
        rwkv_w_v=rwkv_w_rkv[:, 2].astype(BF16),
        rwkv_w0=row3(rwkv_w0), rwkv_w1=rwkv_w1.astype(BF16), rwkv_w2=rwkv_w2.astype(BF16),
        rwkv_a0=row3(rwkv_a0), rwkv_a1=rwkv_a1.astype(BF16), rwkv_a2=rwkv_a2.astype(BF16),
        rwkv_v0=row3(rwkv_v0), rwkv_v1=rwkv_v1.astype(BF16), rwkv_v2=rwkv_v2.astype(BF16),
        rwkv_g1=rwkv_g1.astype(BF16), rwkv_g2=rwkv_g2.astype(BF16),
        rwkv_k_k=row3(rwkv_k_k), rwkv_k_a=row3(rwkv_k_a),
        rwkv_r_k=rwkv_r_k.reshape(n_rwkv, 1, d),
        rwkv_ln_w=row3(rwkv_ln_w), rwkv_ln_b=row3(rwkv_ln_b),
        rwkv_w_o=rwkv_w_o.astype(BF16),
    )
    dtype = x_prompt.dtype
    zero_gdn = jnp.zeros((n_gdn, bsz) + state_gdn.shape[2:], dtype)
    zero_conv = jnp.zeros((n_gdn, bsz) + state_gdn_conv.shape[2:], dtype)
    zero_rwkv = jnp.zeros((n_rwkv, bsz) + state_rwkv.shape[2:], dtype)
    zero_shift = jnp.zeros((n_rwkv, bsz, d), dtype)
    prompt = _trunk(x_prompt.reshape(bsz * t, d), bsz, t, zero_gdn, zero_conv, zero_rwkv, zero_shift, p,
                    chunk=64, nb=2)
    sample = _trunk(x_sample.reshape(dbsz * dt, d), dbsz, dt, state_gdn, state_gdn_conv, state_rwkv,
                    state_rwkv_shift, p, chunk=SUBLANES, nb=4)
    return (prompt[0], sample[0]) + prompt[1:] + sample[1:]
```

```python
import functools

import jax
import jax.numpy as jnp
from jax import lax
from jax.experimental import pallas as pl
from jax.experimental.pallas import tpu as pltpu

F32 = jnp.float32
BF16 = jnp.bfloat16

NORM_EPS = 1e-6
L2_EPS = 1e-6
GN_EPS = 64e-5

GDN_HEADS = 8
GDN_HEAD_DIM = 128
GDN_QKV = 3 * GDN_HEADS * GDN_HEAD_DIM
GDN_CONV = 4
RWKV_HEAD = 64

V7X_VMEM_BYTES = 64 * 1024 * 1024
VMEM_LIMIT_BYTES = V7X_VMEM_BYTES * 7 // 8
SUBLANES = 8
LANES = 128
BF16_ROWS = 16
MXU_TILE = 256


def _params(*semantics):
    return pltpu.CompilerParams(dimension_semantics=semantics, vmem_limit_bytes=VMEM_LIMIT_BYTES)


def _rms(x, g):
    return x * lax.rsqrt(jnp.mean(x * x, axis=-1, keepdims=True) + NORM_EPS) * g


def _softplus(x):
    return jnp.maximum(x, 0.0) + jnp.log1p(jnp.exp(-jnp.abs(x)))


def _dot(a, b):
    return jnp.dot(a.astype(BF16), b.astype(BF16), preferred_element_type=F32)


def _dot_tn(a, b):
    return lax.dot_general(a.astype(BF16), b.astype(BF16), (((0,), (0,)), ((), ())),
                           preferred_element_type=F32)


def _split2(a):
    hi = a.astype(BF16)
    lo = (a - hi.astype(F32)).astype(BF16)
    return hi, lo


def _split3(a):
    hi = a.astype(BF16)
    r = a - hi.astype(F32)
    mid = r.astype(BF16)
    lo = (r - mid.astype(F32)).astype(BF16)
    return hi, mid, lo


def _tri_incl(c):
    ri = lax.broadcasted_iota(jnp.int32, (c, c), 0)
    ci = lax.broadcasted_iota(jnp.int32, (c, c), 1)
    return ri >= ci, ri > ci


def _cumsum_rows(tri_bf, x):
    hi, mid, lo = _split3(x)
    return (jnp.dot(tri_bf, hi, preferred_element_type=F32)
            + (jnp.dot(tri_bf, mid, preferred_element_type=F32) + jnp.dot(tri_bf, lo, preferred_element_type=F32)))


def _log2(n):
    assert n & (n - 1) == 0, n
    return n.bit_length() - 1


def _block_mask(rows, cols, rblk, cblk):
    ri = lax.broadcasted_iota(jnp.int32, (rows, cols), 0)
    ci = lax.broadcasted_iota(jnp.int32, (rows, cols), 1)
    return (ri >> _log2(rblk)) == (ci >> _log2(cblk))


def _tile_rows(x, n):
    return x if n == 1 else jnp.concatenate([x] * n, axis=0)


class _BlockDiag:
    def __init__(self, c, n, cb, ref=None):
        self.c, self.n, self.cb, self.ref = c, n, cb, ref
        self.slot = 0
        self.mask = None if ref is not None else _block_mask(n * c, n * cb, c, cb)

    def __call__(self, x):
        c, n, cb = self.c, self.n, self.cb
        if self.ref is None:
            return jnp.where(self.mask, _tile_rows(x.astype(F32), n), 0.0).astype(BF16)
        slot, self.slot = self.slot, self.slot + 1
        xb = x.astype(BF16)
        for h in range(n):
            self.ref[slot, h * c:(h + 1) * c, h * cb:(h + 1) * cb] = xb[:, h * cb:(h + 1) * cb]
        return self.ref[slot]


def _shared_rhs_dots(xs, rhs):
    m = xs[0].shape[0]
    if len(xs) == 1 or m % BF16_ROWS != 0:
        return [jnp.dot(x.astype(BF16), rhs, preferred_element_type=F32) for x in xs]
    t = jnp.dot(jnp.concatenate([x.astype(BF16) for x in xs], axis=0), rhs, preferred_element_type=F32)
    return [t[i * m:(i + 1) * m] for i in range(len(xs))]


_INVERSE_SLOTS = lambda c: 3 + (1 + _log2(c // SUBLANES) if c > SUBLANES else 0)


def _unit_lower_inverse_wide(mats, c, n, bd):
    ri = lax.broadcasted_iota(jnp.int32, (c, n * c), 0)
    cj = lax.broadcasted_iota(jnp.int32, (c, n * c), 1) & (c - 1)
    eye = (ri == cj).astype(F32)
    base = (ri >> 3) == (cj >> 3)
    idx = range(len(mats))
    nm = [jnp.where(base, -a, 0.0) for a in mats]
    p = [eye + x for x in nm]
    n2 = [_shared_rhs_dots([nm[i]], bd(nm[i]))[0] for i in idx]
    pn = [_shared_rhs_dots([p[i], n2[i]], bd(n2[i])) for i in idx]
    p = [p[i] + pn[i][0] for i in idx]
    p = [p[i] + _shared_rhs_dots([p[i]], bd(pn[i][1]))[0] for i in idx]
    if c > SUBLANES:
        a_bd = [bd(a) for a in mats]
    s = SUBLANES
    while s < c:
        pa = [_shared_rhs_dots([p[i]], a_bd[i])[0] for i in idx]
        e = [_shared_rhs_dots([pa[i]], bd(p[i]))[0] for i in idx]
        shift = _log2(2 * s)
        merge = ((ri >> shift) == (cj >> shift)) & ((ri & s) != 0) & ((cj & s) == 0)
        p = [p[i] - jnp.where(merge, e[i], 0.0) for i in idx]
        s *= 2
    return p


def _block_diag_scratch(chunk, nb, heads, head_dim, *, channel_sites):
    if chunk % BF16_ROWS != 0:
        return []
    hg = min(heads, MXU_TILE // chunk)
    ntg = nb * (heads // hg)
    return [pltpu.VMEM((channel_sites * ntg, hg * chunk, hg * head_dim), BF16),
            pltpu.VMEM((_INVERSE_SLOTS(chunk) * ntg, hg * chunk, hg * chunk), BF16)]


def _gdn_mixer_out(o_ref, z_ref, on_ref, w_ref):
    hd = GDN_HEAD_DIM
    parts = []
    for h in range(GDN_HEADS):
        sl = slice(h * hd, (h + 1) * hd)
        o = o_ref[:, sl]
        z = z_ref[:, sl]
        on = o * lax.rsqrt(jnp.mean(o * o, axis=-1, keepdims=True) + NORM_EPS) * on_ref[...]
        parts.append((on * (z * jax.nn.sigmoid(z))).astype(BF16))
    return jnp.dot(jnp.concatenate(parts, axis=1), w_ref[...], preferred_element_type=F32)


def _rwkv_mixer_out(y_ref, gate_ref, w_ref):
    return jnp.dot((y_ref[...] * gate_ref[...]).astype(BF16), w_ref[...], preferred_element_type=F32)


_MIXER_REFS = {None: 0, "gdn": 5, "rwkv": 4}


def _ffn_body(*refs, mixer, emit_u, tf):
    n_mix = _MIXER_REFS[mixer]
    mix, (h_ref, gin_ref, gout_ref, gnext_ref, wi_ref, wo_ref, hout_ref, *rest) = refs[:n_mix], refs[n_mix:]
    f = wo_ref.shape[0]
    x = h_ref[...]
    if mixer == "gdn":
        x = x + _rms(_gdn_mixer_out(*mix[:4]), mix[4][...])
    elif mixer == "rwkv":
        x = x + _rms(_rwkv_mixer_out(*mix[:3]), mix[3][...])
    xn = _rms(x, gin_ref[...]).astype(BF16)
    acc = None
    for c0 in range(0, f, tf):
        gate = jnp.dot(xn, wi_ref[:, c0:c0 + tf], preferred_element_type=F32)
        up = jnp.dot(xn, wi_ref[:, f + c0:f + c0 + tf], preferred_element_type=F32)
        act = ((gate * jax.nn.sigmoid(gate)) * up).astype(BF16)
        part = jnp.dot(act, wo_ref[c0:c0 + tf, :], preferred_element_type=F32)
        acc = part if acc is None else acc + part
    hn = x + 0.5 * _rms(acc, gout_ref[...])
    hout_ref[...] = hn
    if emit_u:
        rest[0][...] = _rms(hn, gnext_ref[...])


def _ffn(h, g_in, g_out, g_next, w_in, w_out, layer, half, *, emit_u, tm, tf, mixer=None, mixer_args=()):
    n, d = h.shape
    f = w_out.shape[2]
    row = pl.BlockSpec((tm, d), lambda i: (i, 0))
    vec = pl.BlockSpec((1, d), lambda i: (0, 0))
    resident = dict(pipeline_mode=pl.Buffered(1))
    if mixer == "gdn":
        dv = mixer_args[0].shape[1]
        mixer_specs = [pl.BlockSpec((tm, dv), lambda i: (i, 0)),
                       pl.BlockSpec((tm, dv), lambda i: (i, GDN_QKV // dv)),
                       pl.BlockSpec((1, GDN_HEAD_DIM), lambda i: (0, 0)),
                       pl.BlockSpec((dv, d), lambda i: (0, 0), **resident), vec]
    elif mixer == "rwkv":
        mixer_specs = [row, row, pl.BlockSpec((d, d), lambda i: (0, 0), **resident), vec]
    else:
        mixer_specs = []
    assert len(mixer_specs) == len(mixer_args) == _MIXER_REFS[mixer]
    out_shape = [jax.ShapeDtypeStruct((n, d), F32)]
    out_specs = [row]
    if emit_u:
        out_shape.append(jax.ShapeDtypeStruct((n, d), F32))
        out_specs.append(row)
    return pl.pallas_call(
        functools.partial(_ffn_body, mixer=mixer, emit_u=emit_u, tf=tf),
        grid=(n // tm,),
        in_specs=mixer_specs + [row, vec, vec, vec,
                  pl.BlockSpec((None, None, d, 2 * f), lambda i: (layer, half, 0, 0), **resident),
                  pl.BlockSpec((None, None, f, d), lambda i: (layer, half, 0, 0), **resident)],
        out_specs=out_specs,
        out_shape=out_shape,
        compiler_params=_params("parallel"),
        name="ffn_u" if emit_u else "ffn_" + mixer,
    )(*mixer_args, h, g_in, g_out, g_next, w_in, w_out)


def _gdn_proj_body(u_ref, w_ref, wba_ref, main_ref, ba_ref):
    xb = u_ref[...].astype(BF16)
    main_ref[...] = jnp.dot(xb, w_ref[...], preferred_element_type=F32)
    ba_ref[...] = jnp.dot(xb, wba_ref[...], preferred_element_type=F32)


def _gdn_proj(u, w_main, w_ba, *, tm):
    n, d = u.shape
    nm = w_main.shape[1]
    resident = dict(pipeline_mode=pl.Buffered(1))
    return pl.pallas_call(
        _gdn_proj_body,
        grid=(n // tm,),
        in_specs=[pl.BlockSpec((tm, d), lambda i: (i, 0)),
                  pl.BlockSpec((d, nm), lambda i: (0, 0), **resident),
                  pl.BlockSpec((d, LANES), lambda i: (0, 0), **resident)],
        out_specs=[pl.BlockSpec((tm, nm), lambda i: (i, 0)),
                   pl.BlockSpec((tm, LANES), lambda i: (i, 0))],
        out_shape=[jax.ShapeDtypeStruct((n, nm), F32), jax.ShapeDtypeStruct((n, LANES), F32)],
        compiler_params=_params("parallel"),
        name="gdn_proj",
    )(u, w_main, w_ba)


def _gdn_rec_body(cur_ref, prev_ref, init_ref, ba_ref, cw_ref, alr_ref, dbr_ref,
                  s0_ref, *rest, chunk, t_valid, nb, layer, first_writer):
    o_ref, sout_ref, ext_sc, s_sc, *bd_scratch = rest if first_writer else rest[1:]
    sout = sout_ref.at[layer] if first_writer else sout_ref
    c = pl.program_id(1)
    hd = GDN_HEAD_DIM
    nh = GDN_HEADS
    sgh = MXU_TILE // hd
    nsg = nh // sgh

    @pl.when(c == 0)
    def _():
        s_sc[...] = jnp.zeros_like(s_sc)
        for q in range(nb):
            for h in range(nh):
                g, j = divmod(h, sgh)
                s_sc[q * nsg + g, j * hd:(j + 1) * hd, j * hd:(j + 1) * hd] = s0_ref[q, h]
        for ref in bd_scratch:
            ref[...] = jnp.zeros_like(ref)
        ext_sc[:, 0:SUBLANES, :] = init_ref[...]

    @pl.when(c > 0)
    def _():
        ext_sc[:, 0:SUBLANES, :] = prev_ref[...]

    ext_sc[:, SUBLANES:SUBLANES + chunk, :] = cur_ref[...]
    incl, _ = _tri_incl(chunk)
    tri_bf = incl.astype(BF16)

    kb, qn, kn, vb, kbeg, qdec, kdec, gcol, dlast = ([] for _ in range(9))
    for s_ in range(nb):
        first = SUBLANES - (GDN_CONV - 1)
        conv = cw_ref[0:1, :] * ext_sc[s_, first:first + chunk, :]
        for tap in range(1, GDN_CONV):
            conv = conv + cw_ref[tap:tap + 1, :] * ext_sc[s_, first + tap:first + tap + chunk, :]
        qkv = conv * jax.nn.sigmoid(conv)

        ba = ba_ref[s_]
        g_cols = -jnp.exp(alr_ref[...]) * _softplus(ba + dbr_ref[...])
        beta_cols = jax.nn.sigmoid(ba)
        if t_valid < chunk:
            row_ok = lax.broadcasted_iota(jnp.int32, (chunk, 1), 0) < t_valid
            qkv = jnp.where(row_ok, qkv, 0.0)
            g_cols = jnp.where(row_ok, g_cols, 0.0)
        gc_cols = _cumsum_rows(tri_bf, g_cols)

        for h in range(nh):
            q = qkv[:, h * hd:(h + 1) * hd]
            k = qkv[:, (nh + h) * hd:(nh + h + 1) * hd]
            v = qkv[:, (2 * nh + h) * hd:(2 * nh + h + 1) * hd]
            qn_h = q * lax.rsqrt(jnp.sum(q * q, axis=-1, keepdims=True) + L2_EPS) * (hd ** -0.5)
            kn_h = k * lax.rsqrt(jnp.sum(k * k, axis=-1, keepdims=True) + L2_EPS)
            beta = beta_cols[:, h:h + 1]
            gc = gc_cols[:, nh + h:nh + h + 1]
            eg = jnp.exp(gc)
            g_last = gc[chunk - 1:chunk, :]
            kb_h = kn_h * beta
            kb.append(kb_h)
            qn.append(qn_h)
            kn.append(kn_h)
            vb.append(v * beta)
            kbeg.append(kb_h * eg)
            qdec.append(qn_h * eg)
            kdec.append(kn_h * jnp.exp(g_last - gc))
            gcol.append(gc)
            dlast.append(jnp.exp(g_last))

    hg = min(nh, MXU_TILE // chunk)
    ntg = nh // hg
    wt = hg * chunk
    kt = hg * hd
    sg_per_tg = kt // MXU_TILE
    tgs = range(nb * ntg)
    cat = lambda xs, t: jnp.concatenate(xs[t * hg:(t + 1) * hg], axis=1)
    bdc_ref, bdt_ref = bd_scratch if bd_scratch else (None, None)
    bd_c = _BlockDiag(chunk, hg, hd, bdc_ref)
    bd_t = _BlockDiag(chunk, hg, chunk, bdt_ref)
    state_mask = _block_mask(MXU_TILE, MXU_TILE, hd, hd)
    ri = lax.broadcasted_iota(jnp.int32, (chunk, wt), 0)
    cw = lax.broadcasted_iota(jnp.int32, (chunk, wt), 1)
    cj = cw & (chunk - 1)
    strict_w = ri > cj
    incl_w = ri >= cj
    ones_cc = jnp.ones((chunk, chunk), BF16)

    decay = []
    for t in tgs:
        gce = jnp.broadcast_to(gcol[t * hg], (chunk, wt))
        for j in range(1, hg):
            gce = jnp.where(cw >= j * chunk, jnp.broadcast_to(gcol[t * hg + j], (chunk, wt)), gce)
        hi, mid, lo = _split3(jnp.where(ri == cj, gce, 0.0))
        gcj = (jnp.dot(ones_cc, hi, preferred_element_type=F32)
               + (jnp.dot(ones_cc, mid, preferred_element_type=F32) + jnp.dot(ones_cc, lo, preferred_element_type=F32)))
        decay.append(jnp.exp(jnp.where(incl_w, gce - gcj, -jnp.inf)))

    lhs = [jnp.concatenate([cat(kb, t), cat(qn, t)], axis=0).astype(BF16) for t in tgs]
    mk = [lax.dot_general(lhs[t], bd_c(cat(kn, t)), (((1,), (1,)), ((), ())),
                          preferred_element_type=F32) for t in tgs]
    a_mat = [jnp.where(strict_w, mk[t][:chunk] * decay[t], 0.0) for t in tgs]
    attn = [jnp.where(incl_w, mk[t][chunk:] * decay[t], 0.0) for t in tgs]
    t_mat = _unit_lower_inverse_wide(a_mat, chunk, hg, bd_t)
    t_bf = [x.astype(BF16) for x in t_mat]
    u = [jnp.dot(t_bf[t], bd_c(cat(vb, t)), preferred_element_type=F32) for t in tgs]
    w = [jnp.dot(t_bf[t], bd_c(cat(kbeg, t)), preferred_element_type=F32) for t in tgs]

    v_new, qs = [], []
    for t in tgs:
        qd = cat(qdec, t)
        parts = []
        for q_ in range(sg_per_tg):
            sl = slice(q_ * MXU_TILE, (q_ + 1) * MXU_TILE)
            parts.append(_dot(jnp.concatenate([w[t][:, sl], qd[:, sl]], axis=0), s_sc[t * sg_per_tg + q_]))
        ws = parts[0] if sg_per_tg == 1 else jnp.concatenate(parts, axis=1)
        v_new.append(u[t] - ws[:chunk])
        qs.append(ws[chunk:])
    o = [qs[t] + jnp.dot(attn[t].astype(BF16), bd_c(v_new[t]), preferred_element_type=F32)
         for t in tgs]
    join = lambda xs, s_: xs[s_] if ntg == 1 else jnp.concatenate(xs[s_ * ntg:(s_ + 1) * ntg], axis=1)
    lane = lax.broadcasted_iota(jnp.int32, (1, MXU_TILE), 1)
    for s_ in range(nb):
        o_ref[s_] = join(o, s_)
        vn_all = join(v_new, s_)
        for g in range(nsg):
            sl = slice(g * MXU_TILE, (g + 1) * MXU_TILE)
            h0 = s_ * nh + g * sgh
            dl = jnp.broadcast_to(dlast[h0], (1, MXU_TILE))
            for j in range(1, sgh):
                dl = jnp.where(lane >= j * hd, jnp.broadcast_to(dlast[h0 + j], (1, MXU_TILE)), dl)
            upd = _dot_tn(jnp.concatenate(kdec[h0:h0 + sgh], axis=1), vn_all[:, sl])
            s_sc[s_ * nsg + g] = s_sc[s_ * nsg + g] * dl + jnp.where(state_mask, upd, 0.0)

    @pl.when(c == pl.num_programs(1) - 1)
    def _():
        if first_writer:
            sout_ref[...] = jnp.zeros_like(sout_ref)
        for s_ in range(nb):
            for h in range(nh):
                g, j = divmod(h, sgh)
                sout[s_, h] = s_sc[s_ * nsg + g, j * hd:(j + 1) * hd, j * hd:(j + 1) * hd]


def _gdn_rec(qkv, conv_init, ba, conv_w, alr, dbr, s0, s_acc, layer, *, chunk, t_valid, nb):
    bsz, tp, width = qkv.shape
    assert bsz % nb == 0 and tp % chunk == 0
    blk = (nb, GDN_HEADS, GDN_HEAD_DIM, GDN_HEAD_DIM)
    state = pl.BlockSpec((None,) + blk, lambda b, c: (layer, b, 0, 0, 0))
    first = s_acc is None
    out_state = pl.BlockSpec((s0.shape[0],) + blk, lambda b, c: (0, b, 0, 0, 0)) if first else state
    nc = tp // chunk
    cb = chunk // SUBLANES
    dv = GDN_HEADS * GDN_HEAD_DIM
    full2 = lambda shape: pl.BlockSpec(shape, lambda b, c: (0, 0))
    return pl.pallas_call(
        functools.partial(_gdn_rec_body, chunk=chunk, t_valid=t_valid, nb=nb, layer=layer, first_writer=first),
        grid=(bsz // nb, nc),
        in_specs=[pl.BlockSpec((nb, chunk, GDN_QKV), lambda b, c: (b, c, 0)),
                  pl.BlockSpec((nb, SUBLANES, GDN_QKV), lambda b, c: (b, jnp.maximum(c * cb - 1, 0), 0)),
                  pl.BlockSpec((nb, SUBLANES, GDN_QKV), lambda b, c: (b, 0, 0)),
                  pl.BlockSpec((nb, chunk, LANES), lambda b, c: (b, c, 0)),
                  full2(conv_w.shape), full2(alr.shape), full2(dbr.shape),
                  state] + ([] if first else [pl.BlockSpec(memory_space=pl.ANY)]),
        out_specs=[pl.BlockSpec((nb, chunk, dv), lambda b, c: (b, c, 0)), out_state],
        out_shape=[jax.ShapeDtypeStruct((bsz, tp, dv), F32), jax.ShapeDtypeStruct(s0.shape, F32)],
        input_output_aliases={} if first else {8: 1},
        scratch_shapes=[pltpu.VMEM((nb, chunk + SUBLANES, GDN_QKV), F32),
                        pltpu.VMEM((nb * dv // MXU_TILE, MXU_TILE, MXU_TILE), F32)]
        + _block_diag_scratch(chunk, nb, GDN_HEADS, GDN_HEAD_DIM, channel_sites=4),
        compiler_params=_params("parallel", "arbitrary"),
        name="gdn_rec",
    )(qkv, qkv, conv_init, ba, conv_w, alr, dbr, s0, *([] if first else [s_acc]))


def _rwkv_proj_body(*refs, has_vres, seq_len, tm):
    if has_vres:
        (x_ref, prev_ref, start_ref, vf_ref, lerp_ref, wr_ref, wk_ref, wv_ref, w0_ref, w1_ref, w2_ref, a0_ref,
         a1_ref, a2_ref, g1_ref, g2_ref, v0_ref, v1_ref, v2_ref, r_ref, k_ref, v_ref, lw_ref, a_ref, g_ref) = refs
    else:
        (x_ref, prev_ref, start_ref, lerp_ref, wr_ref, wk_ref, wv_ref, w0_ref, w1_ref, w2_ref, a0_ref, a1_ref,
         a2_ref, g1_ref, g2_ref, r_ref, k_ref, v_ref, lw_ref, a_ref, g_ref) = refs
    x = x_ref[...]
    rolled = pltpu.roll(x, 1, axis=0)
    rows = lax.broadcasted_iota(jnp.int32, (tm, 1), 0)
    if tm <= seq_len:
        i = pl.program_id(0)
        tiles_per_seq = seq_len // tm
        carried = start_ref[pl.ds(i // tiles_per_seq, 1), :]
        before = prev_ref[SUBLANES - 1:SUBLANES, :]
        first = jnp.where(i % tiles_per_seq == 0, carried, before)
        xp = jnp.where(rows == 0, first, rolled)
    else:
        xp = jnp.where((rows & (seq_len - 1)) == 0, start_ref[...], rolled)
    dx = xp - x

    def mix(s):
        return (x + dx * lerp_ref[s:s + 1, :]).astype(BF16)

    r_ref[...] = jnp.dot(mix(0), wr_ref[...], preferred_element_type=F32)
    k_ref[...] = jnp.dot(mix(1), wk_ref[...], preferred_element_type=F32)
    xv = mix(2)
    v = jnp.dot(xv, wv_ref[...], preferred_element_type=F32)
    if has_vres:
        gate = jax.nn.sigmoid(v0_ref[...] + _dot(jnp.dot(xv, v1_ref[...], preferred_element_type=F32), v2_ref[...]))
        v = v + (vf_ref[...] - v) * gate
    v_ref[...] = v
    dec = w0_ref[...] + _dot(jnp.tanh(jnp.dot(mix(3), w1_ref[...], preferred_element_type=F32)), w2_ref[...])
    logw = -_softplus(-dec) - 0.5
    lw_ref[...] = -jnp.exp(logw)
    a_ref[...] = jax.nn.sigmoid(a0_ref[...] + _dot(jnp.dot(mix(4), a1_ref[...], preferred_element_type=F32),
                                                   a2_ref[...]))
    g_ref[...] = _dot(jax.nn.sigmoid(jnp.dot(mix(5), g1_ref[...], preferred_element_type=F32)), g2_ref[...])


def _rwkv_proj(x, shift, v_first, lerp, wr, wk, wv, w0, w1, w2, a0, a1, a2, g1, g2, vres, *, seq_len, tm):
    n, d = x.shape
    has_vres = vres is not None
    row = pl.BlockSpec((tm, d), lambda i: (i, 0))
    full = lambda a: pl.BlockSpec(a.shape, lambda i: (0, 0))
    tb = tm // SUBLANES
    if tm <= seq_len:
        assert seq_len % tm == 0
        start, start_spec = shift, full(shift)
    else:
        assert tm % seq_len == 0 and seq_len & (seq_len - 1) == 0
        start, start_spec = jnp.repeat(shift, seq_len, axis=0), row
    prev_spec = pl.BlockSpec((SUBLANES, d), lambda i: (jnp.maximum(i * tb - 1, 0), 0))
    args = [x, x, start] + ([v_first] if has_vres else []) + [lerp, wr, wk, wv, w0, w1, w2, a0, a1, a2, g1, g2]
    n_rows = 4 if has_vres else 3
    in_specs = [row, prev_spec, start_spec] + ([row] if has_vres else []) + [full(a) for a in args[n_rows:]]
    if has_vres:
        args += list(vres)
        in_specs += [full(a) for a in vres]
    return pl.pallas_call(
        functools.partial(_rwkv_proj_body, has_vres=has_vres, seq_len=seq_len, tm=tm),
        grid=(n // tm,),
        in_specs=in_specs,
        out_specs=[row] * 6,
        out_shape=[jax.ShapeDtypeStruct((n, d), F32)] * 6,
        compiler_params=_params("parallel"),
        name="rwkv_proj_vres" if has_vres else "rwkv_proj",
    )(*args)


def _rwkv_rec_body(r_ref, k_ref, v_ref, lw_ref, a_ref, kk_ref, ka_ref, rk_ref, lnw_ref, lnb_ref, s0_ref,
                   *rest, chunk, t_valid, heads, nb, layer, first_writer):
    y_ref, sout_ref, s_sc, *bd_scratch = rest if first_writer else rest[1:]
    sout = sout_ref.at[layer] if first_writer else sout_ref
    c = pl.program_id(1)
    hd = RWKV_HEAD
    d = heads * hd
    gw = MXU_TILE
    sgh = gw // hd
    nsg = heads // sgh
    hg = min(heads, gw // chunk)
    ntg = heads // hg
    kt = hg * hd
    sg_per_tg = kt // gw

    @pl.when(c == 0)
    def _():
        s_sc[...] = jnp.zeros_like(s_sc)
        for q in range(nb):
            for h in range(heads):
                g, j = divmod(h, sgh)
                s_sc[q * nsg + g, j * hd:(j + 1) * hd, j * hd:(j + 1) * hd] = s0_ref[q, h]
        for ref in bd_scratch:
            ref[...] = jnp.zeros_like(ref)

    state_mask = _block_mask(gw, gw, hd, hd)
    ones_bd = state_mask.astype(BF16)

    def head_sums(xs):
        outs = [[] for _ in xs]
        for g in range(nsg):
            parts = []
            for x in xs:
                parts.extend(_split2(x[:, g * gw:(g + 1) * gw]))
            sums = _shared_rhs_dots(parts, ones_bd)
            for i in range(len(xs)):
                outs[i].append(sums[2 * i] + sums[2 * i + 1])
        return [jnp.concatenate(o, axis=1) for o in outs]

    incl, _ = _tri_incl(chunk)
    tri_bf = incl.astype(BF16)

    def prep(q):
        r_all = r_ref[q]
        k_all = k_ref[q]
        v_all = v_ref[q]
        lw_all = lw_ref[q]
        a_all = a_ref[q]
        if t_valid < chunk:
            row_ok = lax.broadcasted_iota(jnp.int32, (chunk, 1), 0) < t_valid
            r_all = jnp.where(row_ok, r_all, 0.0)
            k_all = jnp.where(row_ok, k_all, 0.0)
            v_all = jnp.where(row_ok, v_all, 0.0)
            lw_all = jnp.where(row_ok, lw_all, 0.0)
        gc = _cumsum_rows(tri_bf, lw_all)
        kx = k_all * kk_ref[...]
        kmod = k_all * (1.0 + (a_all - 1.0) * ka_ref[...])
        kx_ss, rk_sum = head_sums([kx * kx, r_all * kmod * rk_ref[...]])
        kk = kx * lax.rsqrt(kx_ss + L2_EPS)
        bv = kk * a_all
        e_neg = jnp.exp(-gc)
        g_last = gc[chunk - 1:chunk, :]
        e_rem = jnp.exp(g_last - gc)
        return dict(v=v_all, rk_sum=rk_sum, at=-kk * jnp.exp(gc - lw_all), rt=r_all * jnp.exp(gc),
                    bt=bv * e_neg, kt=kmod * e_neg, bp=bv * e_rem, kp=kmod * e_rem, dec=jnp.exp(g_last))

    seqs = [prep(q) for q in range(nb)]

    bdc_ref, bdt_ref = bd_scratch if bd_scratch else (None, None)
    bd_c = _BlockDiag(chunk, hg, hd, bdc_ref)
    bd_t = _BlockDiag(chunk, hg, chunk, bdt_ref)
    ri = lax.broadcasted_iota(jnp.int32, (chunk, hg * chunk), 0)
    cj = lax.broadcasted_iota(jnp.int32, (chunk, hg * chunk), 1) & (chunk - 1)
    strict_w = ri > cj
    incl_w = ri >= cj

    tgs = range(nb * ntg)
    lane = lambda name, t: seqs[t // ntg][name][:, (t % ntg) * kt:(t % ntg + 1) * kt]
    lhs = [jnp.concatenate([lane("at", t), lane("rt", t)], axis=0).astype(BF16) for t in tgs]
    mb = [lax.dot_general(lhs[t], bd_c(lane("bt", t)), (((1,), (1,)), ((), ())),
                          preferred_element_type=F32) for t in tgs]
    mk = [lax.dot_general(lhs[t], bd_c(lane("kt", t)), (((1,), (1,)), ((), ())),
                          preferred_element_type=F32) for t in tgs]
    a_ab = [jnp.where(strict_w, mb[t][:chunk], 0.0) for t in tgs]
    a_rb = [jnp.where(incl_w, mb[t][chunk:], 0.0) for t in tgs]
    a_ak = [jnp.where(strict_w, mk[t][:chunk], 0.0) for t in tgs]
    a_rk = [jnp.where(incl_w, mk[t][chunk:], 0.0) for t in tgs]
    t_mat = _unit_lower_inverse_wide([-x for x in a_ab], chunk, hg, bd_t)

    xs = []
    for t in tgs:
        parts = []
        for q in range(sg_per_tg):
            g = t * sg_per_tg + q
            parts.append(lax.dot_general(lhs[t][:, q * gw:(q + 1) * gw], s_sc[g].astype(BF16),
                                         (((1,), (1,)), ((), ())), preferred_element_type=F32))
        xs.append(parts[0] if len(parts) == 1 else jnp.concatenate(parts, axis=1))
    av = [_shared_rhs_dots([a_ak[t], a_rk[t]], bd_c(lane("v", t))) for t in tgs]
    rhs_u = [xs[t][:chunk] + av[t][0] for t in tgs]
    u = [jnp.dot(t_mat[t].astype(BF16), bd_c(rhs_u[t]), preferred_element_type=F32) for t in tgs]
    y = [xs[t][chunk:] + jnp.dot(a_rb[t].astype(BF16), bd_c(u[t]), preferred_element_type=F32) + av[t][1]
         for t in tgs]

    join = lambda xs_, q: xs_[q] if ntg == 1 else jnp.concatenate(xs_[q * ntg:(q + 1) * ntg], axis=1)
    for q in range(nb):
        sq = seqs[q]
        u_all = join(u, q)
        for g in range(nsg):
            sl = slice(g * gw, (g + 1) * gw)
            upd = _dot_tn(jnp.concatenate([u_all[:, sl], sq["v"][:, sl]], axis=0),
                          jnp.concatenate([sq["bp"][:, sl], sq["kp"][:, sl]], axis=0))
            s_sc[q * nsg + g] = s_sc[q * nsg + g] * sq["dec"][:, sl] + jnp.where(state_mask, upd, 0.0)

    y_all = [join(y, q) for q in range(nb)]
    yc = [y_all[q] - head_sums([y_all[q]])[0] * (1.0 / hd) for q in range(nb)]
    var = [head_sums([yc[q] * yc[q]])[0] * (1.0 / hd) for q in range(nb)]
    for q in range(nb):
        yn = yc[q] * lax.rsqrt(var[q] + GN_EPS)
        y_ref[q] = yn * lnw_ref[...] + lnb_ref[...] + seqs[q]["rk_sum"] * seqs[q]["v"]

    @pl.when(c == pl.num_programs(1) - 1)
    def _():
        if first_writer:
            sout_ref[...] = jnp.zeros_like(sout_ref)
        for q in range(nb):
            for h in range(heads):
                g, j = divmod(h, sgh)
                sout[q, h] = s_sc[q * nsg + g, j * hd:(j + 1) * hd, j * hd:(j + 1) * hd]


def _rwkv_rec(r, k, v, lw, a, k_k, k_a, r_k, ln_w, ln_b, s0, s_acc, layer, *, chunk, t_valid, nb):
    bsz, tp, d = r.shape
    assert bsz % nb == 0 and tp % chunk == 0
    heads = d // RWKV_HEAD
    nc = tp // chunk
    seq = pl.BlockSpec((nb, chunk, d), lambda b, c: (b, c, 0))
    vec = pl.BlockSpec((1, d), lambda b, c: (0, 0))
    blk = (nb, heads, RWKV_HEAD, RWKV_HEAD)
    st = pl.BlockSpec((None,) + blk, lambda b, c: (layer, b, 0, 0, 0))
    first = s_acc is None
    out_st = pl.BlockSpec((s0.shape[0],) + blk, lambda b, c: (0, b, 0, 0, 0)) if first else st
    return pl.pallas_call(
        functools.partial(_rwkv_rec_body, chunk=chunk, t_valid=t_valid, heads=heads, nb=nb, layer=layer,
                          first_writer=first),
        grid=(bsz // nb, nc),
        in_specs=[seq] * 5 + [vec] * 5 + [st] + ([] if first else [pl.BlockSpec(memory_space=pl.ANY)]),
        out_specs=[seq, out_st],
        out_shape=[jax.ShapeDtypeStruct((bsz, tp, d), F32), jax.ShapeDtypeStruct(s0.shape, F32)],
        input_output_aliases={} if first else {11: 1},
        scratch_shapes=[pltpu.VMEM((nb * d // MXU_TILE, MXU_TILE, MXU_TILE), F32)]
        + _block_diag_scratch(chunk, nb, heads, RWKV_HEAD, channel_sites=5),
        compiler_params=_params("parallel", "arbitrary"),
        name="rwkv_rec",
    )(r, k, v, lw, a, k_k, k_a, r_k, ln_w, ln_b, s0, *([] if first else [s_acc]))


def _pad_time(x, tp):
    t = x.shape[1]
    if t == tp:
        return x
    return jnp.pad(x, [(0, 0), (0, tp - t)] + [(0, 0)] * (x.ndim - 2))


def _tiles(n):
    return 1024 if n % 1024 == 0 else n


def _trunk(h, bsz, t, gdn_s, gdn_buf, rw_s, rw_shift, p, chunk, nb):
    n, d = h.shape
    depth = p["norm_g"].shape[0]
    tm = _tiles(n)
    tp = -(-t // chunk) * chunk
    out_gdn_buf, out_rw_shift = [], []
    out_gdn_s = out_rw_s = None
    v_first = None
    for i in range(depth):
        ng = p["norm_g"][i]
        row = lambda s: ng[s][None, :]
        h, u = _ffn(h, row(0), row(1), row(2), p["ffn_w_in"], p["ffn_w_out"], i, 0,
                    emit_u=True, tm=min(tm, 512), tf=256)
        j = i // 2
        if i % 2 == 0:
            main, ba = _gdn_proj(u, p["gdn_w_main"][j], p["gdn_w_ba"][j], tm=min(tm, 512))
            main3 = main.reshape(bsz, t, main.shape[1])
            out_gdn_buf.append(main3[:, t - (GDN_CONV - 1):, :GDN_QKV])
            ba3 = _pad_time(ba.reshape(bsz, t, LANES), tp)
            conv_init = jnp.pad(gdn_buf[j], [(0, 0), (SUBLANES - (GDN_CONV - 1), 0), (0, 0)])
            qkv = main3 if tp == t else _pad_time(main3[:, :, :GDN_QKV], tp)
            o, out_gdn_s = _gdn_rec(qkv, conv_init, ba3, p["gdn_conv_w"][j], p["gdn_alr"][j], p["gdn_dbr"][j],
                                    gdn_s, out_gdn_s, j, chunk=chunk, t_valid=min(t, chunk), nb=nb)
            o2 = o[:, :t].reshape(n, o.shape[2])
            mixer, mixer_args = "gdn", (o2, main, p["gdn_o_norm"][j], p["gdn_w_out"][j], row(3))
        else:
            u3 = u.reshape(bsz, t, d)
            out_rw_shift.append(u3[:, -1])
            vres = None if j == 0 else (p["rwkv_v0"][j - 1], p["rwkv_v1"][j - 1], p["rwkv_v2"][j - 1])
            r, k, v, lw, a, g = _rwkv_proj(
                u, rw_shift[j], v_first, p["rwkv_lerp"][j], p["rwkv_w_r"][j], p["rwkv_w_k"][j], p["rwkv_w_v"][j],
                p["rwkv_w0"][j], p["rwkv_w1"][j], p["rwkv_w2"][j], p["rwkv_a0"][j], p["rwkv_a1"][j],
                p["rwkv_a2"][j], p["rwkv_g1"][j], p["rwkv_g2"][j], vres, seq_len=t, tm=min(tm, 256))
            if v_first is None:
                v_first = v
            seq = lambda x: _pad_time(x.reshape(bsz, t, d), tp)
            y, out_rw_s = _rwkv_rec(seq(r), seq(k), seq(v), seq(lw), seq(a), p["rwkv_k_k"][j], p["rwkv_k_a"][j],
                                    p["rwkv_r_k"][j], p["rwkv_ln_w"][j], p["rwkv_ln_b"][j], rw_s, out_rw_s, j,
                                    chunk=chunk, t_valid=min(t, chunk), nb=nb)
            mixer, mixer_args = "rwkv", (y[:, :t].reshape(n, d), g, p["rwkv_w_o"][j], row(3))
        h = _ffn(h, row(4), row(5), row(5), p["ffn_w_in"], p["ffn_w_out"], i, 1,
                 emit_u=False, tm=min(tm, 512), tf=256, mixer=mixer, mixer_args=mixer_args)[0]
    return h.reshape(bsz, t, d), out_gdn_s, jnp.stack(out_gdn_buf), out_rw_s, jnp.stack(out_rw_shift)


def kernel(x_prompt, x_sample, state_gdn, state_gdn_conv, state_rwkv, state_rwkv_shift, norm_g, ffn_w_in, ffn_w_out, gdn_w_in, gdn_conv_w, gdn_a_log, gdn_dt_bias, gdn_o_norm, gdn_w_out, rwkv_lerp, rwkv_w_rkv, rwkv_w0, rwkv_w1, rwkv_w2, rwkv_a0, rwkv_a1, rwkv_a2, rwkv_v0, rwkv_v1, rwkv_v2, rwkv_g1, rwkv_g2, rwkv_k_k, rwkv_k_a, rwkv_r_k, rwkv_ln_w, rwkv_ln_b, rwkv_w_o):
    bsz, t, d = x_prompt.shape
    dbsz, dt, _ = x_sample.shape
    n_gdn = gdn_w_in.shape[0]
    n_rwkv = rwkv_w_rkv.shape[0]
    nh = GDN_HEADS
    n_main = GDN_QKV + nh * GDN_HEAD_DIM

    def lane_slot(x, offset):
        return jnp.pad(x, [(0, 0), (offset, LANES - offset - nh)])[:, None, :]

    row3 = lambda x: x[:, None, :]
    p = dict(
        norm_g=norm_g,
        ffn_w_in=ffn_w_in.astype(BF16),
        ffn_w_out=ffn_w_out.astype(BF16),
        gdn_w_main=gdn_w_in[:, :, :n_main].astype(BF16),
        gdn_w_ba=jnp.pad(gdn_w_in[:, :, n_main:], [(0, 0), (0, 0), (0, LANES - 2 * nh)]).astype(BF16),
        gdn_conv_w=gdn_conv_w,
        gdn_alr=lane_slot(gdn_a_log, nh), gdn_dbr=lane_slot(gdn_dt_bias, nh),
        gdn_o_norm=row3(gdn_o_norm),
        gdn_w_out=gdn_w_out.astype(BF16),
        rwkv_lerp=jnp.pad(rwkv_lerp, [(0, 0), (0, SUBLANES - rwkv_lerp.shape[1]), (0, 0)]),
        rwkv_w_r=rwkv_w_rkv[:, 0].astype(BF16), rwkv_w_k=rwkv_w_rkv[:, 1].astype(BF16),
        rwkv_w_v=rwkv_w_rkv[:, 2].astype(BF16),
        rwkv_w0=row3(rwkv_w0), rwkv_w1=rwkv_w1.astype(BF16), rwkv_w2=rwkv_w2.astype(BF16),
        rwkv_a0=row3(rwkv_a0), rwkv_a1=rwkv_a1.astype(BF16), rwkv_a2=rwkv_a2.astype(BF16),
        rwkv_v0=row3(rwkv_v0), rwkv_v1=rwkv_v1.astype(BF16), rwkv_v2=rwkv_v2.astype(BF16),
        rwkv_g1=rwkv_g1.astype(BF16), rwkv_g2=rwkv_g2.astype(BF16),
        rwkv_k_k=row3(rwkv_k_k), rwkv_k_a=row3(rwkv_k_a),
        rwkv_r_k=rwkv_r_k.reshape(n_rwkv, 1, d),
        rwkv_ln_w=row3(rwkv_ln_w), rwkv_ln_b=row3(rwkv_ln_b),
        rwkv_w_o=rwkv_w_o.astype(BF16),
    )
    dtype = x_prompt.dtype
    zero_gdn = jnp.zeros((n_gdn, bsz) + state_gdn.shape[2:], dtype)
    zero_conv = jnp.zeros((n_gdn, bsz) + state_gdn_conv.shape[2:], dtype)
    zero_rwkv = jnp.zeros((n_rwkv, bsz) + state_rwkv.shape[2:], dtype)
    zero_shift = jnp.zeros((n_rwkv, bsz, d), dtype)
    prompt = _trunk(x_prompt.reshape(bsz * t, d), bsz, t, zero_gdn, zero_conv, zero_rwkv, zero_shift, p,
                    chunk=64, nb=2)
    sample = _trunk(x_sample.reshape(dbsz * dt, d), dbsz, dt, state_gdn, state_gdn_conv, state_rwkv,
                    state_rwkv_shift, p, chunk=SUBLANES, nb=4)
    return (prompt[0], sample[0]) + prompt[1:] + sample[1:]
```

```python
import functools

import jax
import jax.numpy as jnp
from jax import lax
from jax.experimental import pallas as pl
from jax.experimental.pallas import tpu as pltpu

F32 = jnp.float32
BF16 = jnp.bfloat16

NORM_EPS = 1e-6
L2_EPS = 1e-6
GN_EPS = 64e-5

GDN_HEADS = 8
GDN_HEAD_DIM = 128
GDN_QKV = 3 * GDN_HEADS * GDN_HEAD_DIM
GDN_CONV = 4
RWKV_HEAD = 64

V7X_VMEM_BYTES = 64 * 1024 * 1024
VMEM_LIMIT_BYTES = V7X_VMEM_BYTES * 7 // 8
SUBLANES = 8
LANES = 128
BF16_ROWS = 16
MXU_TILE = 256


def _params(*semantics):
    return pltpu.CompilerParams(dimension_semantics=semantics, vmem_limit_bytes=VMEM_LIMIT_BYTES)


class _Layer:
    def __init__(self, arr, *idx):
        self.arr, self.idx = arr, idx

    @property
    def shape(self):
        return self.arr.shape[len(self.idx):]


def _whole(x, grid_rank, **spec_kwargs):
    arr, idx = (x.arr, x.idx) if isinstance(x, _Layer) else (x, ())
    index = idx + (0,) * (arr.ndim - len(idx))
    index_map = (lambda i: index) if grid_rank == 1 else (lambda i, j: index)
    return arr, pl.BlockSpec((None,) * len(idx) + arr.shape[len(idx):], index_map, **spec_kwargs)


def _rms(x, g):
    return x * lax.rsqrt(jnp.mean(x * x, axis=-1, keepdims=True) + NORM_EPS) * g


def _softplus(x):
    return jnp.maximum(x, 0.0) + jnp.log1p(jnp.exp(-jnp.abs(x)))


def _dot(a, b):
    return jnp.dot(a.astype(BF16), b.astype(BF16), preferred_element_type=F32)


def _dot_tn(a, b):
    return lax.dot_general(a.astype(BF16), b.astype(BF16), (((0,), (0,)), ((), ())),
                           preferred_element_type=F32)


def _split2(a):
    hi = a.astype(BF16)
    lo = (a - hi.astype(F32)).astype(BF16)
    return hi, lo


def _split3(a):
    hi = a.astype(BF16)
    r = a - hi.astype(F32)
    mid = r.astype(BF16)
    lo = (r - mid.astype(F32)).astype(BF16)
    return hi, mid, lo


def _tri_incl(c):
    ri = lax.broadcasted_iota(jnp.int32, (c, c), 0)
    ci = lax.broadcasted_iota(jnp.int32, (c, c), 1)
    return ri >= ci, ri > ci


def _cumsum_rows(tri_bf, x):
    hi, mid, lo = _split3(x)
    return (jnp.dot(tri_bf, hi, preferred_element_type=F32)
            + (jnp.dot(tri_bf, mid, preferred_element_type=F32) + jnp.dot(tri_bf, lo, preferred_element_type=F32)))


def _log2(n):
    assert n & (n - 1) == 0, n
    return n.bit_length() - 1


def _block_mask(rows, cols, rblk, cblk):
    ri = lax.broadcasted_iota(jnp.int32, (rows, cols), 0)
    ci = lax.broadcasted_iota(jnp.int32, (rows, cols), 1)
    return (ri >> _log2(rblk)) == (ci >> _log2(cblk))


def _tile_rows(x, n):
    return x if n == 1 else jnp.concatenate([x] * n, axis=0)


class _BlockDiag:
    def __init__(self, c, n, cb, ref=None):
        self.c, self.n, self.cb, self.ref = c, n, cb, ref
        self.slot = 0
        self.mask = None if ref is not None else _block_mask(n * c, n * cb, c, cb)

    def __call__(self, x):
        c, n, cb = self.c, self.n, self.cb
        if self.ref is None:
            return jnp.where(self.mask, _tile_rows(x.astype(F32), n), 0.0).astype(BF16)
        slot, self.slot = self.slot, self.slot + 1
        xb = x.astype(BF16)
        for h in range(n):
            self.ref[slot, h * c:(h + 1) * c, h * cb:(h + 1) * cb] = xb[:, h * cb:(h + 1) * cb]
        return self.ref[slot]


def _shared_rhs_dots(xs, rhs):
    m = xs[0].shape[0]
    if len(xs) == 1 or m % BF16_ROWS != 0:
        return [jnp.dot(x.astype(BF16), rhs, preferred_element_type=F32) for x in xs]
    t = jnp.dot(jnp.concatenate([x.astype(BF16) for x in xs], axis=0), rhs, preferred_element_type=F32)
    return [t[i * m:(i + 1) * m] for i in range(len(xs))]


_INVERSE_SLOTS = lambda c: 3 + (1 + _log2(c // SUBLANES) if c > SUBLANES else 0)


def _unit_lower_inverse_wide(mats, c, n, bd):
    ri = lax.broadcasted_iota(jnp.int32, (c, n * c), 0)
    cj = lax.broadcasted_iota(jnp.int32, (c, n * c), 1) & (c - 1)
    eye = (ri == cj).astype(F32)
    base = (ri >> 3) == (cj >> 3)
    idx = range(len(mats))
    nm = [jnp.where(base, -a, 0.0) for a in mats]
    p = [eye + x for x in nm]
    n2 = [_shared_rhs_dots([nm[i]], bd(nm[i]))[0] for i in idx]
    pn = [_shared_rhs_dots([p[i], n2[i]], bd(n2[i])) for i in idx]
    p = [p[i] + pn[i][0] for i in idx]
    p = [p[i] + _shared_rhs_dots([p[i]], bd(pn[i][1]))[0] for i in idx]
    if c > SUBLANES:
        a_bd = [bd(a) for a in mats]
    s = SUBLANES
    while s < c:
        pa = [_shared_rhs_dots([p[i]], a_bd[i])[0] for i in idx]
        e = [_shared_rhs_dots([pa[i]], bd(p[i]))[0] for i in idx]
        shift = _log2(2 * s)
        merge = ((ri >> shift) == (cj >> shift)) & ((ri & s) != 0) & ((cj & s) == 0)
        p = [p[i] - jnp.where(merge, e[i], 0.0) for i in idx]
        s *= 2
    return p


def _block_diag_scratch(chunk, nb, heads, head_dim, *, channel_sites):
    if chunk % BF16_ROWS != 0:
        return []
    hg = min(heads, MXU_TILE // chunk)
    ntg = nb * (heads // hg)
    return [pltpu.VMEM((channel_sites * ntg, hg * chunk, hg * head_dim), BF16),
            pltpu.VMEM((_INVERSE_SLOTS(chunk) * ntg, hg * chunk, hg * chunk), BF16)]


def _gdn_mixer_out(o_ref, z_ref, on_ref, w_ref):
    hd = GDN_HEAD_DIM
    parts = []
    for h in range(GDN_HEADS):
        sl = slice(h * hd, (h + 1) * hd)
        o = o_ref[:, sl]
        z = z_ref[:, sl]
        on = o * lax.rsqrt(jnp.mean(o * o, axis=-1, keepdims=True) + NORM_EPS) * on_ref[...]
        parts.append((on * (z * jax.nn.sigmoid(z))).astype(BF16))
    return jnp.dot(jnp.concatenate(parts, axis=1), w_ref[...], preferred_element_type=F32)


def _rwkv_mixer_out(y_ref, gate_ref, w_ref):
    return jnp.dot((y_ref[...] * gate_ref[...]).astype(BF16), w_ref[...], preferred_element_type=F32)


_MIXER_REFS = {None: 0, "gdn": 5, "rwkv": 4}


def _ffn_body(*refs, mixer, emit_u, tf):
    n_mix = _MIXER_REFS[mixer]
    mix, (h_ref, gin_ref, gout_ref, gnext_ref, wi_ref, wo_ref, hout_ref, *rest) = refs[:n_mix], refs[n_mix:]
    f = wo_ref.shape[0]
    x = h_ref[...]
    if mixer == "gdn":
        x = x + _rms(_gdn_mixer_out(*mix[:4]), mix[4][...])
    elif mixer == "rwkv":
        x = x + _rms(_rwkv_mixer_out(*mix[:3]), mix[3][...])
    xn = _rms(x, gin_ref[...]).astype(BF16)
    acc = None
    for c0 in range(0, f, tf):
        gate = jnp.dot(xn, wi_ref[:, c0:c0 + tf], preferred_element_type=F32)
        up = jnp.dot(xn, wi_ref[:, f + c0:f + c0 + tf], preferred_element_type=F32)
        act = ((gate * jax.nn.sigmoid(gate)) * up).astype(BF16)
        part = jnp.dot(act, wo_ref[c0:c0 + tf, :], preferred_element_type=F32)
        acc = part if acc is None else acc + part
    hn = x + 0.5 * _rms(acc, gout_ref[...])
    hout_ref[...] = hn
    if emit_u:
        rest[0][...] = _rms(hn, gnext_ref[...])


def _ffn(h, g_in, g_out, g_next, w_in, w_out, layer, half, *, emit_u, tm, tf, mixer=None, mixer_args=()):
    n, d = h.shape
    f = w_out.shape[2]
    row = pl.BlockSpec((tm, d), lambda i: (i, 0))
    resident = dict(pipeline_mode=pl.Buffered(1))
    if mixer == "gdn":
        o, main, o_norm, w_mix, g_mix = mixer_args
        dv = o.shape[1]
        mixer_ops = [(o, pl.BlockSpec((tm, dv), lambda i: (i, 0))),
                     (main, pl.BlockSpec((tm, dv), lambda i: (i, GDN_QKV // dv))),
                     _whole(o_norm, 1), _whole(w_mix, 1, **resident), _whole(g_mix, 1)]
    elif mixer == "rwkv":
        y, gate, w_mix, g_mix = mixer_args
        mixer_ops = [(y, row), (gate, row), _whole(w_mix, 1, **resident), _whole(g_mix, 1)]
    else:
        mixer_ops = []
    assert len(mixer_ops) == _MIXER_REFS[mixer]
    ops = mixer_ops + [(h, row), _whole(g_in, 1), _whole(g_out, 1), _whole(g_next, 1),
                       (w_in, pl.BlockSpec((None, None, d, 2 * f), lambda i: (layer, half, 0, 0), **resident)),
                       (w_out, pl.BlockSpec((None, None, f, d), lambda i: (layer, half, 0, 0), **resident))]
    out_shape = [jax.ShapeDtypeStruct((n, d), F32)]
    out_specs = [row]
    if emit_u:
        out_shape.append(jax.ShapeDtypeStruct((n, d), F32))
        out_specs.append(row)
    return pl.pallas_call(
        functools.partial(_ffn_body, mixer=mixer, emit_u=emit_u, tf=tf),
        grid=(n // tm,),
        in_specs=[spec for _, spec in ops],
        out_specs=out_specs,
        out_shape=out_shape,
        compiler_params=_params("parallel"),
        name="ffn_u" if emit_u else "ffn_" + mixer,
    )(*[op for op, _ in ops])


def _gdn_proj_body(u_ref, w_ref, wba_ref, main_ref, ba_ref):
    xb = u_ref[...].astype(BF16)
    main_ref[...] = jnp.dot(xb, w_ref[...], preferred_element_type=F32)
    ba_ref[...] = jnp.dot(xb, wba_ref[...], preferred_element_type=F32)


def _gdn_proj(u, w_main, w_ba, *, tm):
    n, d = u.shape
    nm = w_main.shape[1]
    resident = dict(pipeline_mode=pl.Buffered(1))
    w_main, w_main_spec = _whole(w_main, 1, **resident)
    w_ba, w_ba_spec = _whole(w_ba, 1, **resident)
    return pl.pallas_call(
        _gdn_proj_body,
        grid=(n // tm,),
        in_specs=[pl.BlockSpec((tm, d), lambda i: (i, 0)), w_main_spec, w_ba_spec],
        out_specs=[pl.BlockSpec((tm, nm), lambda i: (i, 0)),
                   pl.BlockSpec((tm, LANES), lambda i: (i, 0))],
        out_shape=[jax.ShapeDtypeStruct((n, nm), F32), jax.ShapeDtypeStruct((n, LANES), F32)],
        compiler_params=_params("parallel"),
        name="gdn_proj",
    )(u, w_main, w_ba)


def _gdn_rec_body(cur_ref, prev_ref, init_ref, ba_ref, cw_ref, alr_ref, dbr_ref,
                  s0_ref, *rest, chunk, t_valid, nb, layer, first_writer):
    o_ref, sout_ref, ext_sc, s_sc, *bd_scratch = rest if first_writer else rest[1:]
    sout = sout_ref.at[layer] if first_writer else sout_ref
    c = pl.program_id(1)
    hd = GDN_HEAD_DIM
    nh = GDN_HEADS
    sgh = MXU_TILE // hd
    nsg = nh // sgh

    @pl.when(c == 0)
    def _():
        s_sc[...] = jnp.zeros_like(s_sc)
        for q in range(nb):
            for h in range(nh):
                g, j = divmod(h, sgh)
                s_sc[q * nsg + g, j * hd:(j + 1) * hd, j * hd:(j + 1) * hd] = s0_ref[q, h]
        for ref in bd_scratch:
            ref[...] = jnp.zeros_like(ref)
        ext_sc[:, 0:SUBLANES, :] = init_ref[...]

    @pl.when(c > 0)
    def _():
        ext_sc[:, 0:SUBLANES, :] = prev_ref[...]

    ext_sc[:, SUBLANES:SUBLANES + chunk, :] = cur_ref[...]
    incl, _ = _tri_incl(chunk)
    tri_bf = incl.astype(BF16)

    kb, qn, kn, vb, kbeg, qdec, kdec, gcol, dlast = ([] for _ in range(9))
    for s_ in range(nb):
        first = SUBLANES - (GDN_CONV - 1)
        conv = cw_ref[0:1, :] * ext_sc[s_, first:first + chunk, :]
        for tap in range(1, GDN_CONV):
            conv = conv + cw_ref[tap:tap + 1, :] * ext_sc[s_, first + tap:first + tap + chunk, :]
        qkv = conv * jax.nn.sigmoid(conv)

        ba = ba_ref[s_]
        g_cols = -jnp.exp(alr_ref[...]) * _softplus(ba + dbr_ref[...])
        beta_cols = jax.nn.sigmoid(ba)
        if t_valid < chunk:
            row_ok = lax.broadcasted_iota(jnp.int32, (chunk, 1), 0) < t_valid
            qkv = jnp.where(row_ok, qkv, 0.0)
            g_cols = jnp.where(row_ok, g_cols, 0.0)
        gc_cols = _cumsum_rows(tri_bf, g_cols)

        for h in range(nh):
            q = qkv[:, h * hd:(h + 1) * hd]
            k = qkv[:, (nh + h) * hd:(nh + h + 1) * hd]
            v = qkv[:, (2 * nh + h) * hd:(2 * nh + h + 1) * hd]
            qn_h = q * lax.rsqrt(jnp.sum(q * q, axis=-1, keepdims=True) + L2_EPS) * (hd ** -0.5)
            kn_h = k * lax.rsqrt(jnp.sum(k * k, axis=-1, keepdims=True) + L2_EPS)
            beta = beta_cols[:, h:h + 1]
            gc = gc_cols[:, nh + h:nh + h + 1]
            eg = jnp.exp(gc)
            g_last = gc[chunk - 1:chunk, :]
            kb_h = kn_h * beta
            kb.append(kb_h)
            qn.append(qn_h)
            kn.append(kn_h)
            vb.append(v * beta)
            kbeg.append(kb_h * eg)
            qdec.append(qn_h * eg)
            kdec.append(kn_h * jnp.exp(g_last - gc))
            gcol.append(gc)
            dlast.append(jnp.exp(g_last))

    hg = min(nh, MXU_TILE // chunk)
    ntg = nh // hg
    wt = hg * chunk
    kt = hg * hd
    sg_per_tg = kt // MXU_TILE
    tgs = range(nb * ntg)
    cat = lambda xs, t: jnp.concatenate(xs[t * hg:(t + 1) * hg], axis=1)
    bdc_ref, bdt_ref = bd_scratch if bd_scratch else (None, None)
    bd_c = _BlockDiag(chunk, hg, hd, bdc_ref)
    bd_t = _BlockDiag(chunk, hg, chunk, bdt_ref)
    state_mask = _block_mask(MXU_TILE, MXU_TILE, hd, hd)
    ri = lax.broadcasted_iota(jnp.int32, (chunk, wt), 0)
    cw = lax.broadcasted_iota(jnp.int32, (chunk, wt), 1)
    cj = cw & (chunk - 1)
    strict_w = ri > cj
    incl_w = ri >= cj
    ones_cc = jnp.ones((chunk, chunk), BF16)

    decay = []
    for t in tgs:
        gce = jnp.broadcast_to(gcol[t * hg], (chunk, wt))
        for j in range(1, hg):
            gce = jnp.where(cw >= j * chunk, jnp.broadcast_to(gcol[t * hg + j], (chunk, wt)), gce)
        hi, mid, lo = _split3(jnp.where(ri == cj, gce, 0.0))
        gcj = (jnp.dot(ones_cc, hi, preferred_element_type=F32)
               + (jnp.dot(ones_cc, mid, preferred_element_type=F32) + jnp.dot(ones_cc, lo, preferred_element_type=F32)))
        decay.append(jnp.exp(jnp.where(incl_w, gce - gcj, -jnp.inf)))

    lhs = [jnp.concatenate([cat(kb, t), cat(qn, t)], axis=0).astype(BF16) for t in tgs]
    mk = [lax.dot_general(lhs[t], bd_c(cat(kn, t)), (((1,), (1,)), ((), ())),
                          preferred_element_type=F32) for t in tgs]
    a_mat = [jnp.where(strict_w, mk[t][:chunk] * decay[t], 0.0) for t in tgs]
    attn = [jnp.where(incl_w, mk[t][chunk:] * decay[t], 0.0) for t in tgs]
    t_mat = _unit_lower_inverse_wide(a_mat, chunk, hg, bd_t)
    t_bf = [x.astype(BF16) for x in t_mat]
    u = [jnp.dot(t_bf[t], bd_c(cat(vb, t)), preferred_element_type=F32) for t in tgs]
    w = [jnp.dot(t_bf[t], bd_c(cat(kbeg, t)), preferred_element_type=F32) for t in tgs]

    v_new, qs = [], []
    for t in tgs:
        qd = cat(qdec, t)
        parts = []
        for q_ in range(sg_per_tg):
            sl = slice(q_ * MXU_TILE, (q_ + 1) * MXU_TILE)
            parts.append(_dot(jnp.concatenate([w[t][:, sl], qd[:, sl]], axis=0), s_sc[t * sg_per_tg + q_]))
        ws = parts[0] if sg_per_tg == 1 else jnp.concatenate(parts, axis=1)
        v_new.append(u[t] - ws[:chunk])
        qs.append(ws[chunk:])
    o = [qs[t] + jnp.dot(attn[t].astype(BF16), bd_c(v_new[t]), preferred_element_type=F32)
         for t in tgs]
    join = lambda xs, s_: xs[s_] if ntg == 1 else jnp.concatenate(xs[s_ * ntg:(s_ + 1) * ntg], axis=1)
    lane = lax.broadcasted_iota(jnp.int32, (1, MXU_TILE), 1)
    for s_ in range(nb):
        o_ref[s_] = join(o, s_)
        vn_all = join(v_new, s_)
        for g in range(nsg):
            sl = slice(g * MXU_TILE, (g + 1) * MXU_TILE)
            h0 = s_ * nh + g * sgh
            dl = jnp.broadcast_to(dlast[h0], (1, MXU_TILE))
            for j in range(1, sgh):
                dl = jnp.where(lane >= j * hd, jnp.broadcast_to(dlast[h0 + j], (1, MXU_TILE)), dl)
            upd = _dot_tn(jnp.concatenate(kdec[h0:h0 + sgh], axis=1), vn_all[:, sl])
            s_sc[s_ * nsg + g] = s_sc[s_ * nsg + g] * dl + jnp.where(state_mask, upd, 0.0)

    @pl.when(c == pl.num_programs(1) - 1)
    def _():
        if first_writer:
            sout_ref[...] = jnp.zeros_like(sout_ref)
        for s_ in range(nb):
            for h in range(nh):
                g, j = divmod(h, sgh)
                sout[s_, h] = s_sc[s_ * nsg + g, j * hd:(j + 1) * hd, j * hd:(j + 1) * hd]


def _gdn_rec(qkv, conv_init, ba, conv_w, alr, dbr, s0, s_acc, layer, *, chunk, t_valid, nb):
    bsz, tp, width = qkv.shape
    assert bsz % nb == 0 and tp % chunk == 0
    blk = (nb, GDN_HEADS, GDN_HEAD_DIM, GDN_HEAD_DIM)
    state = pl.BlockSpec((None,) + blk, lambda b, c: (layer, b, 0, 0, 0))
    first = s_acc is None
    out_state = pl.BlockSpec((s0.shape[0],) + blk, lambda b, c: (0, b, 0, 0, 0)) if first else state
    nc = tp // chunk
    cb = chunk // SUBLANES
    dv = GDN_HEADS * GDN_HEAD_DIM
    (conv_w, conv_w_spec), (alr, alr_spec), (dbr, dbr_spec) = (_whole(x, 2) for x in (conv_w, alr, dbr))
    return pl.pallas_call(
        functools.partial(_gdn_rec_body, chunk=chunk, t_valid=t_valid, nb=nb, layer=layer, first_writer=first),
        grid=(bsz // nb, nc),
        in_specs=[pl.BlockSpec((nb, chunk, GDN_QKV), lambda b, c: (b, c, 0)),
                  pl.BlockSpec((nb, SUBLANES, GDN_QKV), lambda b, c: (b, jnp.maximum(c * cb - 1, 0), 0)),
                  pl.BlockSpec((nb, SUBLANES, GDN_QKV), lambda b, c: (b, 0, 0)),
                  pl.BlockSpec((nb, chunk, LANES), lambda b, c: (b, c, 0)),
                  conv_w_spec, alr_spec, dbr_spec,
                  state] + ([] if first else [pl.BlockSpec(memory_space=pl.ANY)]),
        out_specs=[pl.BlockSpec((nb, chunk, dv), lambda b, c: (b, c, 0)), out_state],
        out_shape=[jax.ShapeDtypeStruct((bsz, tp, dv), F32), jax.ShapeDtypeStruct(s0.shape, F32)],
        input_output_aliases={} if first else {8: 1},
        scratch_shapes=[pltpu.VMEM((nb, chunk + SUBLANES, GDN_QKV), F32),
                        pltpu.VMEM((nb * dv // MXU_TILE, MXU_TILE, MXU_TILE), F32)]
        + _block_diag_scratch(chunk, nb, GDN_HEADS, GDN_HEAD_DIM, channel_sites=4),
        compiler_params=_params("parallel", "arbitrary"),
        name="gdn_rec",
    )(qkv, qkv, conv_init, ba, conv_w, alr, dbr, s0, *([] if first else [s_acc]))


def _rwkv_proj_body(*refs, has_vres, seq_len, tm):
    if has_vres:
        (x_ref, prev_ref, start_ref, vf_ref, lerp_ref, wr_ref, wk_ref, wv_ref, w0_ref, w1_ref, w2_ref, a0_ref,
         a1_ref, a2_ref, g1_ref, g2_ref, v0_ref, v1_ref, v2_ref, r_ref, k_ref, v_ref, lw_ref, a_ref, g_ref) = refs
    else:
        (x_ref, prev_ref, start_ref, lerp_ref, wr_ref, wk_ref, wv_ref, w0_ref, w1_ref, w2_ref, a0_ref, a1_ref,
         a2_ref, g1_ref, g2_ref, r_ref, k_ref, v_ref, lw_ref, a_ref, g_ref) = refs
    x = x_ref[...]
    rolled = pltpu.roll(x, 1, axis=0)
    rows = lax.broadcasted_iota(jnp.int32, (tm, 1), 0)
    if tm <= seq_len:
        i = pl.program_id(0)
        tiles_per_seq = seq_len // tm
        carried = start_ref[pl.ds(i // tiles_per_seq, 1), :]
        before = prev_ref[SUBLANES - 1:SUBLANES, :]
        first = jnp.where(i % tiles_per_seq == 0, carried, before)
        xp = jnp.where(rows == 0, first, rolled)
    else:
        xp = jnp.where((rows & (seq_len - 1)) == 0, start_ref[...], rolled)
    dx = xp - x

    def mix(s):
        return (x + dx * lerp_ref[s:s + 1, :]).astype(BF16)

    r_ref[...] = jnp.dot(mix(0), wr_ref[...], preferred_element_type=F32)
    k_ref[...] = jnp.dot(mix(1), wk_ref[...], preferred_element_type=F32)
    xv = mix(2)
    v = jnp.dot(xv, wv_ref[...], preferred_element_type=F32)
    if has_vres:
        gate = jax.nn.sigmoid(v0_ref[...] + _dot(jnp.dot(xv, v1_ref[...], preferred_element_type=F32), v2_ref[...]))
        v = v + (vf_ref[...] - v) * gate
    v_ref[...] = v
    dec = w0_ref[...] + _dot(jnp.tanh(jnp.dot(mix(3), w1_ref[...], preferred_element_type=F32)), w2_ref[...])
    logw = -_softplus(-dec) - 0.5
    lw_ref[...] = -jnp.exp(logw)
    a_ref[...] = jax.nn.sigmoid(a0_ref[...] + _dot(jnp.dot(mix(4), a1_ref[...], preferred_element_type=F32),
                                                   a2_ref[...]))
    g_ref[...] = _dot(jax.nn.sigmoid(jnp.dot(mix(5), g1_ref[...], preferred_element_type=F32)), g2_ref[...])


def _rwkv_proj(x, shift, v_first, lerp, wr, wk, wv, w0, w1, w2, a0, a1, a2, g1, g2, vres, *, seq_len, tm):
    n, d = x.shape
    has_vres = vres is not None
    row = pl.BlockSpec((tm, d), lambda i: (i, 0))
    full = lambda a: _whole(a, 1, pipeline_mode=pl.Buffered(1))
    tb = tm // SUBLANES
    if tm <= seq_len:
        assert seq_len % tm == 0
        start = full(shift)
    else:
        assert tm % seq_len == 0 and seq_len & (seq_len - 1) == 0
        start = (jnp.repeat(shift.arr[shift.idx], seq_len, axis=0), row)
    prev_spec = pl.BlockSpec((SUBLANES, d), lambda i: (jnp.maximum(i * tb - 1, 0), 0))
    ops = ([(x, row), (x, prev_spec), start] + ([(v_first, row)] if has_vres else [])
           + [full(a) for a in (lerp, wr, wk, wv, w0, w1, w2, a0, a1, a2, g1, g2) + (tuple(vres) if has_vres else ())])
    args = [op for op, _ in ops]
    in_specs = [spec for _, spec in ops]
    return pl.pallas_call(
        functools.partial(_rwkv_proj_body, has_vres=has_vres, seq_len=seq_len, tm=tm),
        grid=(n // tm,),
        in_specs=in_specs,
        out_specs=[row] * 6,
        out_shape=[jax.ShapeDtypeStruct((n, d), F32)] * 6,
        compiler_params=_params("parallel"),
        name="rwkv_proj_vres" if has_vres else "rwkv_proj",
    )(*args)


def _rwkv_rec_body(r_ref, k_ref, v_ref, lw_ref, a_ref, kk_ref, ka_ref, rk_ref, lnw_ref, lnb_ref, s0_ref,
                   *rest, chunk, t_valid, heads, nb, layer, first_writer):
    y_ref, sout_ref, s_sc, *bd_scratch = rest if first_writer else rest[1:]
    sout = sout_ref.at[layer] if first_writer else sout_ref
    c = pl.program_id(1)
    hd = RWKV_HEAD
    d = heads * hd
    gw = MXU_TILE
    sgh = gw // hd
    nsg = heads // sgh
    hg = min(heads, gw // chunk)
    ntg = heads // hg
    kt = hg * hd
    sg_per_tg = kt // gw

    @pl.when(c == 0)
    def _():
        s_sc[...] = jnp.zeros_like(s_sc)
        for q in range(nb):
            for h in range(heads):
                g, j = divmod(h, sgh)
                s_sc[q * nsg + g, j * hd:(j + 1) * hd, j * hd:(j + 1) * hd] = s0_ref[q, h]
        for ref in bd_scratch:
            ref[...] = jnp.zeros_like(ref)

    state_mask = _block_mask(gw, gw, hd, hd)
    ones_bd = state_mask.astype(BF16)

    def head_sums(xs):
        parts = []
        for x in xs:
            for g in range(nsg):
                parts.extend(_split2(x[:, g * gw:(g + 1) * gw]))
        sums = _shared_rhs_dots(parts, ones_bd)
        return [jnp.concatenate([sums[2 * (i * nsg + g)] + sums[2 * (i * nsg + g) + 1] for g in range(nsg)], axis=1)
                for i in range(len(xs))]

    incl, _ = _tri_incl(chunk)
    tri_bf = incl.astype(BF16)

    def load(q):
        r_all = r_ref[q]
        k_all = k_ref[q]
        v_all = v_ref[q]
        lw_all = lw_ref[q]
        a_all = a_ref[q]
        if t_valid < chunk:
            row_ok = lax.broadcasted_iota(jnp.int32, (chunk, 1), 0) < t_valid
            r_all = jnp.where(row_ok, r_all, 0.0)
            k_all = jnp.where(row_ok, k_all, 0.0)
            v_all = jnp.where(row_ok, v_all, 0.0)
            lw_all = jnp.where(row_ok, lw_all, 0.0)
        kx = k_all * kk_ref[...]
        kmod = k_all * (1.0 + (a_all - 1.0) * ka_ref[...])
        return dict(r=r_all, v=v_all, lw=lw_all, a=a_all, kx=kx, kmod=kmod)

    seqs = [load(q) for q in range(nb)]
    sums = head_sums([s["kx"] * s["kx"] for s in seqs] + [s["r"] * s["kmod"] * rk_ref[...] for s in seqs])

    def prep(q):
        s = seqs[q]
        gc = _cumsum_rows(tri_bf, s["lw"])
        kk = s["kx"] * lax.rsqrt(sums[q] + L2_EPS)
        bv = kk * s["a"]
        e_neg = jnp.exp(-gc)
        g_last = gc[chunk - 1:chunk, :]
        e_rem = jnp.exp(g_last - gc)
        return dict(v=s["v"], rk_sum=sums[nb + q], at=-kk * jnp.exp(gc - s["lw"]), rt=s["r"] * jnp.exp(gc),
                    bt=bv * e_neg, kt=s["kmod"] * e_neg, bp=bv * e_rem, kp=s["kmod"] * e_rem, dec=jnp.exp(g_last))

    seqs = [prep(q) for q in range(nb)]

    bdc_ref, bdt_ref = bd_scratch if bd_scratch else (None, None)
    bd_c = _BlockDiag(chunk, hg, hd, bdc_ref)
    bd_t = _BlockDiag(chunk, hg, chunk, bdt_ref)
    ri = lax.broadcasted_iota(jnp.int32, (chunk, hg * chunk), 0)
    cj = lax.broadcasted_iota(jnp.int32, (chunk, hg * chunk), 1) & (chunk - 1)
    strict_w = ri > cj
    incl_w = ri >= cj

    tgs = range(nb * ntg)
    lane = lambda name, t: seqs[t // ntg][name][:, (t % ntg) * kt:(t % ntg + 1) * kt]
    lhs = [jnp.concatenate([lane("at", t), lane("rt", t)], axis=0).astype(BF16) for t in tgs]
    mb = [lax.dot_general(lhs[t], bd_c(lane("bt", t)), (((1,), (1,)), ((), ())),
                          preferred_element_type=F32) for t in tgs]
    mk = [lax.dot_general(lhs[t], bd_c(lane("kt", t)), (((1,), (1,)), ((), ())),
                          preferred_element_type=F32) for t in tgs]
    a_ab = [jnp.where(strict_w, mb[t][:chunk], 0.0) for t in tgs]
    a_rb = [jnp.where(incl_w, mb[t][chunk:], 0.0) for t in tgs]
    a_ak = [jnp.where(strict_w, mk[t][:chunk], 0.0) for t in tgs]
    a_rk = [jnp.where(incl_w, mk[t][chunk:], 0.0) for t in tgs]
    t_mat = _unit_lower_inverse_wide([-x for x in a_ab], chunk, hg, bd_t)

    xs = []
    for t in tgs:
        parts = []
        for q in range(sg_per_tg):
            g = t * sg_per_tg + q
            parts.append(lax.dot_general(lhs[t][:, q * gw:(q + 1) * gw], s_sc[g].astype(BF16),
                                         (((1,), (1,)), ((), ())), preferred_element_type=F32))
        xs.append(parts[0] if len(parts) == 1 else jnp.concatenate(parts, axis=1))
    av = [_shared_rhs_dots([a_ak[t], a_rk[t]], bd_c(lane("v", t))) for t in tgs]
    rhs_u = [xs[t][:chunk] + av[t][0] for t in tgs]
    u = [jnp.dot(t_mat[t].astype(BF16), bd_c(rhs_u[t]), preferred_element_type=F32) for t in tgs]
    y = [xs[t][chunk:] + jnp.dot(a_rb[t].astype(BF16), bd_c(u[t]), preferred_element_type=F32) + av[t][1]
         for t in tgs]

    join = lambda xs_, q: xs_[q] if ntg == 1 else jnp.concatenate(xs_[q * ntg:(q + 1) * ntg], axis=1)
    for q in range(nb):
        sq = seqs[q]
        u_all = join(u, q)
        for g in range(nsg):
            sl = slice(g * gw, (g + 1) * gw)
            upd = _dot_tn(jnp.concatenate([u_all[:, sl], sq["v"][:, sl]], axis=0),
                          jnp.concatenate([sq["bp"][:, sl], sq["kp"][:, sl]], axis=0))
            s_sc[q * nsg + g] = s_sc[q * nsg + g] * sq["dec"][:, sl] + jnp.where(state_mask, upd, 0.0)

    y_all = [join(y, q) for q in range(nb)]
    mean = head_sums(y_all)
    yc = [y_all[q] - mean[q] * (1.0 / hd) for q in range(nb)]
    var = head_sums([x * x for x in yc])
    for q in range(nb):
        yn = yc[q] * lax.rsqrt(var[q] * (1.0 / hd) + GN_EPS)
        y_ref[q] = yn * lnw_ref[...] + lnb_ref[...] + seqs[q]["rk_sum"] * seqs[q]["v"]

    @pl.when(c == pl.num_programs(1) - 1)
    def _():
        if first_writer:
            sout_ref[...] = jnp.zeros_like(sout_ref)
        for q in range(nb):
            for h in range(heads):
                g, j = divmod(h, sgh)
                sout[q, h] = s_sc[q * nsg + g, j * hd:(j + 1) * hd, j * hd:(j + 1) * hd]


def _rwkv_rec(r, k, v, lw, a, k_k, k_a, r_k, ln_w, ln_b, s0, s_acc, layer, *, chunk, t_valid, nb):
    bsz, tp, d = r.shape
    assert bsz % nb == 0 and tp % chunk == 0
    heads = d // RWKV_HEAD
    nc = tp // chunk
    seq = pl.BlockSpec((nb, chunk, d), lambda b, c: (b, c, 0))
    vecs = [_whole(x, 2) for x in (k_k, k_a, r_k, ln_w, ln_b)]
    blk = (nb, heads, RWKV_HEAD, RWKV_HEAD)
    st = pl.BlockSpec((None,) + blk, lambda b, c: (layer, b, 0, 0, 0))
    first = s_acc is None
    out_st = pl.BlockSpec((s0.shape[0],) + blk, lambda b, c: (0, b, 0, 0, 0)) if first else st
    return pl.pallas_call(
        functools.partial(_rwkv_rec_body, chunk=chunk, t_valid=t_valid, heads=heads, nb=nb, layer=layer,
                          first_writer=first),
        grid=(bsz // nb, nc),
        in_specs=[seq] * 5 + [spec for _, spec in vecs] + [st] + ([] if first else [pl.BlockSpec(memory_space=pl.ANY)]),
        out_specs=[seq, out_st],
        out_shape=[jax.ShapeDtypeStruct((bsz, tp, d), F32), jax.ShapeDtypeStruct(s0.shape, F32)],
        input_output_aliases={} if first else {11: 1},
        scratch_shapes=[pltpu.VMEM((nb * d // MXU_TILE, MXU_TILE, MXU_TILE), F32)]
        + _block_diag_scratch(chunk, nb, heads, RWKV_HEAD, channel_sites=5),
        compiler_params=_params("parallel", "arbitrary"),
        name="rwkv_rec",
    )(r, k, v, lw, a, *[op for op, _ in vecs], s0, *([] if first else [s_acc]))


def _pad_time(x, tp):
    t = x.shape[1]
    if t == tp:
        return x
    return jnp.pad(x, [(0, 0), (0, tp - t)] + [(0, 0)] * (x.ndim - 2))


def _tiles(n):
    return 1024 if n % 1024 == 0 else n


def _trunk(h, bsz, t, gdn_s, gdn_buf, rw_s, rw_shift, p, chunk, nb):
    n, d = h.shape
    depth = p["norm_g"].shape[0]
    tm = _tiles(n)
    tp = -(-t // chunk) * chunk
    out_gdn_buf, out_rw_shift = [], []
    out_gdn_s = out_rw_s = None
    v_first = None
    for i in range(depth):
        row = lambda s: _Layer(p["norm_g"], i, s)
        h, u = _ffn(h, row(0), row(1), row(2), p["ffn_w_in"], p["ffn_w_out"], i, 0,
                    emit_u=True, tm=min(tm, 512), tf=256)
        j = i // 2
        if i % 2 == 0:
            at = lambda name: _Layer(p[name], j)
            main, ba = _gdn_proj(u, at("gdn_w_main"), at("gdn_w_ba"), tm=min(tm, 512))
            main3 = main.reshape(bsz, t, main.shape[1])
            out_gdn_buf.append(main3[:, t - (GDN_CONV - 1):, :GDN_QKV])
            ba3 = _pad_time(ba.reshape(bsz, t, LANES), tp)
            conv_init = jnp.pad(gdn_buf[j], [(0, 0), (SUBLANES - (GDN_CONV - 1), 0), (0, 0)])
            qkv = main3 if tp == t else _pad_time(main3[:, :, :GDN_QKV], tp)
            o, out_gdn_s = _gdn_rec(qkv, conv_init, ba3, at("gdn_conv_w"), at("gdn_alr"), at("gdn_dbr"),
                                    gdn_s, out_gdn_s, j, chunk=chunk, t_valid=min(t, chunk), nb=nb)
            o2 = o[:, :t].reshape(n, o.shape[2])
            mixer, mixer_args = "gdn", (o2, main, at("gdn_o_norm"), at("gdn_w_out"), row(3))
        else:
            u3 = u.reshape(bsz, t, d)
            out_rw_shift.append(u3[:, -1])
            at = lambda name: _Layer(p["rwkv_" + name], j)
            vres = None if j == 0 else tuple(_Layer(p["rwkv_" + name], j - 1) for name in ("v0", "v1", "v2"))
            w_rkv = [_Layer(p["rwkv_w_rkv"], j, s) for s in range(3)]
            r, k, v, lw, a, g = _rwkv_proj(
                u, _Layer(rw_shift, j), v_first, at("lerp"), *w_rkv, at("w0"), at("w1"), at("w2"), at("a0"),
                at("a1"), at("a2"), at("g1"), at("g2"), vres, seq_len=t, tm=min(tm, 512))
            if v_first is None:
                v_first = v
            seq = lambda x: _pad_time(x.reshape(bsz, t, d), tp)
            y, out_rw_s = _rwkv_rec(seq(r), seq(k), seq(v), seq(lw), seq(a), at("k_k"), at("k_a"),
                                    at("r_k"), at("ln_w"), at("ln_b"), rw_s, out_rw_s, j,
                                    chunk=chunk, t_valid=min(t, chunk), nb=nb)
            mixer, mixer_args = "rwkv", (y[:, :t].reshape(n, d), g, at("w_o"), row(3))
        h = _ffn(h, row(4), row(5), row(5), p["ffn_w_in"], p["ffn_w_out"], i, 1,
                 emit_u=False, tm=min(tm, 512), tf=256, mixer=mixer, mixer_args=mixer_args)[0]
    return h.reshape(bsz, t, d), out_gdn_s, jnp.stack(out_gdn_buf), out_rw_s, jnp.stack(out_rw_shift)


def kernel(x_prompt, x_sample, state_gdn, state_gdn_conv, state_rwkv, state_rwkv_shift, norm_g, ffn_w_in, ffn_w_out, gdn_w_in, gdn_conv_w, gdn_a_log, gdn_dt_bias, gdn_o_norm, gdn_w_out, rwkv_lerp, rwkv_w_rkv, rwkv_w0, rwkv_w1, rwkv_w2, rwkv_a0, rwkv_a1, rwkv_a2, rwkv_v0, rwkv_v1, rwkv_v2, rwkv_g1, rwkv_g2, rwkv_k_k, rwkv_k_a, rwkv_r_k, rwkv_ln_w, rwkv_ln_b, rwkv_w_o):
    bsz, t, d = x_prompt.shape
    dbsz, dt, _ = x_sample.shape
    n_gdn = gdn_w_in.shape[0]
    n_rwkv = rwkv_w_rkv.shape[0]
    nh = GDN_HEADS
    n_main = GDN_QKV + nh * GDN_HEAD_DIM

    def lane_slot(x, offset):
        return jnp.pad(x, [(0, 0), (offset, LANES - offset - nh)])[:, None, :]

    row3 = lambda x: x[:, None, :]
    p = dict(
        norm_g=norm_g[:, :, None, :],
        ffn_w_in=ffn_w_in.astype(BF16),
        ffn_w_out=ffn_w_out.astype(BF16),
        gdn_w_main=gdn_w_in[:, :, :n_main].astype(BF16),
        gdn_w_ba=jnp.pad(gdn_w_in[:, :, n_main:], [(0, 0), (0, 0), (0, LANES - 2 * nh)]).astype(BF16),
        gdn_conv_w=gdn_conv_w,
        gdn_alr=lane_slot(gdn_a_log, nh), gdn_dbr=lane_slot(gdn_dt_bias, nh),
        gdn_o_norm=row3(gdn_o_norm),
        gdn_w_out=gdn_w_out.astype(BF16),
        rwkv_lerp=jnp.pad(rwkv_lerp, [(0, 0), (0, SUBLANES - rwkv_lerp.shape[1]), (0, 0)]),
        rwkv_w_rkv=rwkv_w_rkv.astype(BF16),
        rwkv_w0=row3(rwkv_w0), rwkv_w1=rwkv_w1.astype(BF16), rwkv_w2=rwkv_w2.astype(BF16),
        rwkv_a0=row3(rwkv_a0), rwkv_a1=rwkv_a1.astype(BF16), rwkv_a2=rwkv_a2.astype(BF16),
        rwkv_v0=row3(rwkv_v0), rwkv_v1=rwkv_v1.astype(BF16), rwkv_v2=rwkv_v2.astype(BF16),
        rwkv_g1=rwkv_g1.astype(BF16), rwkv_g2=rwkv_g2.astype(BF16),
        rwkv_k_k=row3(rwkv_k_k), rwkv_k_a=row3(rwkv_k_a),
        rwkv_r_k=rwkv_r_k.reshape(n_rwkv, 1, d),
        rwkv_ln_w=row3(rwkv_ln_w), rwkv_ln_b=row3(rwkv_ln_b),
        rwkv_w_o=rwkv_w_o.astype(BF16),
    )
    dtype = x_prompt.dtype
    zero_gdn = jnp.zeros((n_gdn, bsz) + state_gdn.shape[2:], dtype)
    zero_conv = jnp.zeros((n_gdn, bsz) + state_gdn_conv.shape[2:], dtype)
    zero_rwkv = jnp.zeros((n_rwkv, bsz) + state_rwkv.shape[2:], dtype)
    zero_shift = jnp.zeros((n_rwkv, bsz, d), dtype)
    prompt = _trunk(x_prompt.reshape(bsz * t, d), bsz, t, zero_gdn, zero_conv, zero_rwkv, zero_shift, p,
                    chunk=64, nb=2)
    sample = _trunk(x_sample.reshape(dbsz * dt, d), dbsz, dt, state_gdn, state_gdn_conv, state_rwkv,
                    state_rwkv_shift, p, chunk=SUBLANES, nb=4)
    return (prompt[0], sample[0]) + prompt[1:] + sample[1:]
```

```python
import functools
import math

import jax
import jax.numpy as jnp
from jax import lax
from jax.experimental import pallas as pl
from jax.experimental.pallas import tpu as pltpu

F32 = jnp.float32
BF16 = jnp.bfloat16

NORM_EPS = 1e-6
L2_EPS = 1e-6
GN_EPS = 64e-5

GDN_HEADS = 8
GDN_HEAD_DIM = 128
GDN_QKV = 3 * GDN_HEADS * GDN_HEAD_DIM
GDN_CONV = 4
RWKV_HEAD = 64

V7X_VMEM_BYTES = 64 * 1024 * 1024
VMEM_LIMIT_BYTES = V7X_VMEM_BYTES * 7 // 8
SUBLANES = 8
LANES = 128
BF16_ROWS = 16
MXU_TILE = 256


def _params(*semantics):
    return pltpu.CompilerParams(dimension_semantics=semantics, vmem_limit_bytes=VMEM_LIMIT_BYTES)


class _Layer:
    def __init__(self, arr, *idx):
        self.arr, self.idx = arr, idx

    @property
    def shape(self):
        return self.arr.shape[len(self.idx):]


def _whole(x, grid_rank, **spec_kwargs):
    arr, idx = (x.arr, x.idx) if isinstance(x, _Layer) else (x, ())
    index = idx + (0,) * (arr.ndim - len(idx))
    index_map = (lambda i: index) if grid_rank == 1 else (lambda i, j: index)
    return arr, pl.BlockSpec((None,) * len(idx) + arr.shape[len(idx):], index_map, **spec_kwargs)


def _rms(x, g):
    return x * lax.rsqrt(jnp.mean(x * x, axis=-1, keepdims=True) + NORM_EPS) * g


def _softplus(x):
    return jnp.maximum(x, 0.0) + jnp.log1p(jnp.exp(-jnp.abs(x)))


def _dot(a, b):
    return jnp.dot(a.astype(BF16), b.astype(BF16), preferred_element_type=F32)


def _dot_tn(a, b):
    return lax.dot_general(a.astype(BF16), b.astype(BF16), (((0,), (0,)), ((), ())),
                           preferred_element_type=F32)


def _split2(a):
    hi = a.astype(BF16)
    lo = (a - hi.astype(F32)).astype(BF16)
    return hi, lo


def _split3(a):
    hi = a.astype(BF16)
    r = a - hi.astype(F32)
    mid = r.astype(BF16)
    lo = (r - mid.astype(F32)).astype(BF16)
    return hi, mid, lo


def _tri_incl(c):
    ri = lax.broadcasted_iota(jnp.int32, (c, c), 0)
    ci = lax.broadcasted_iota(jnp.int32, (c, c), 1)
    return ri >= ci, ri > ci


def _cumsum_rows(tri_bf, x):
    hi, mid, lo = _split3(x)
    return (jnp.dot(tri_bf, hi, preferred_element_type=F32)
            + (jnp.dot(tri_bf, mid, preferred_element_type=F32) + jnp.dot(tri_bf, lo, preferred_element_type=F32)))


def _log2(n):
    assert n & (n - 1) == 0, n
    return n.bit_length() - 1


def _block_mask(rows, cols, rblk, cblk):
    ri = lax.broadcasted_iota(jnp.int32, (rows, cols), 0)
    ci = lax.broadcasted_iota(jnp.int32, (rows, cols), 1)
    return (ri >> _log2(rblk)) == (ci >> _log2(cblk))


def _tile_rows(x, n):
    return x if n == 1 else jnp.concatenate([x] * n, axis=0)


class _BlockDiag:
    def __init__(self, c, n, cb, ref=None):
        self.c, self.n, self.cb, self.ref = c, n, cb, ref
        self.slot = 0
        self.mask = None if ref is not None else _block_mask(n * c, n * cb, c, cb)

    def __call__(self, x):
        c, n, cb = self.c, self.n, self.cb
        if self.ref is None:
            return jnp.where(self.mask, _tile_rows(x.astype(F32), n), 0.0).astype(BF16)
        slot, self.slot = self.slot, self.slot + 1
        xb = x.astype(BF16)
        for h in range(n):
            self.ref[slot, h * c:(h + 1) * c, h * cb:(h + 1) * cb] = xb[:, h * cb:(h + 1) * cb]
        return self.ref[slot]


def _shared_rhs_dots(xs, rhs):
    m = xs[0].shape[0]
    if len(xs) == 1 or m % BF16_ROWS != 0:
        return [jnp.dot(x.astype(BF16), rhs, preferred_element_type=F32) for x in xs]
    t = jnp.dot(jnp.concatenate([x.astype(BF16) for x in xs], axis=0), rhs, preferred_element_type=F32)
    return [t[i * m:(i + 1) * m] for i in range(len(xs))]


_INVERSE_SLOTS = lambda c: 3 + (1 + _log2(c // SUBLANES) if c > SUBLANES else 0)


def _unit_lower_inverse_wide(mats, c, n, bd):
    ri = lax.broadcasted_iota(jnp.int32, (c, n * c), 0)
    cj = lax.broadcasted_iota(jnp.int32, (c, n * c), 1) & (c - 1)
    eye = (ri == cj).astype(F32)
    base = (ri >> 3) == (cj >> 3)
    idx = range(len(mats))
    nm = [jnp.where(base, -a, 0.0) for a in mats]
    p = [eye + x for x in nm]
    n2 = [_shared_rhs_dots([nm[i]], bd(nm[i]))[0] for i in idx]
    pn = [_shared_rhs_dots([p[i], n2[i]], bd(n2[i])) for i in idx]
    p = [p[i] + pn[i][0] for i in idx]
    p = [p[i] + _shared_rhs_dots([p[i]], bd(pn[i][1]))[0] for i in idx]
    if c > SUBLANES:
        a_bd = [bd(a) for a in mats]
    s = SUBLANES
    while s < c:
        pa = [_shared_rhs_dots([p[i]], a_bd[i])[0] for i in idx]
        e = [_shared_rhs_dots([pa[i]], bd(p[i]))[0] for i in idx]
        shift = _log2(2 * s)
        merge = ((ri >> shift) == (cj >> shift)) & ((ri & s) != 0) & ((cj & s) == 0)
        p = [p[i] - jnp.where(merge, e[i], 0.0) for i in idx]
        s *= 2
    return p


def _block_diag_scratch(chunk, nb, heads, head_dim, *, channel_sites):
    if chunk % BF16_ROWS != 0:
        return []
    hg = min(heads, MXU_TILE // chunk)
    ntg = nb * (heads // hg)
    return [pltpu.VMEM((channel_sites * ntg, hg * chunk, hg * head_dim), BF16),
            pltpu.VMEM((_INVERSE_SLOTS(chunk) * ntg, hg * chunk, hg * chunk), BF16)]


def _gdn_mixer_out(o_ref, z_ref, on_ref, w_ref):
    hd = GDN_HEAD_DIM
    parts = []
    for h in range(GDN_HEADS):
        sl = slice(h * hd, (h + 1) * hd)
        o = o_ref[:, sl]
        z = z_ref[:, sl]
        on = o * lax.rsqrt(jnp.mean(o * o, axis=-1, keepdims=True) + NORM_EPS) * on_ref[...]
        parts.append((on * (z * jax.nn.sigmoid(z))).astype(BF16))
    return jnp.dot(jnp.concatenate(parts, axis=1), w_ref[...], preferred_element_type=F32)


def _rwkv_mixer_out(y_ref, gate_ref, w_ref):
    return jnp.dot((y_ref[...] * gate_ref[...]).astype(BF16), w_ref[...], preferred_element_type=F32)


_MIXER_REFS = {None: 0, "gdn": 5, "rwkv": 4}


def _ffn_body(*refs, mixer, emit_u, tf):
    n_mix = _MIXER_REFS[mixer]
    mix, (h_ref, gin_ref, gout_ref, gnext_ref, wi_ref, wo_ref, hout_ref, *rest) = refs[:n_mix], refs[n_mix:]
    f = wo_ref.shape[0]
    x = h_ref[...]
    if mixer == "gdn":
        x = x + _rms(_gdn_mixer_out(*mix[:4]), mix[4][...])
    elif mixer == "rwkv":
        x = x + _rms(_rwkv_mixer_out(*mix[:3]), mix[3][...])
    xn = _rms(x, gin_ref[...]).astype(BF16)
    acc = None
    for c0 in range(0, f, tf):
        gate = jnp.dot(xn, wi_ref[:, c0:c0 + tf], preferred_element_type=F32)
        up = jnp.dot(xn, wi_ref[:, f + c0:f + c0 + tf], preferred_element_type=F32)
        act = ((gate * jax.nn.sigmoid(gate)) * up).astype(BF16)
        part = jnp.dot(act, wo_ref[c0:c0 + tf, :], preferred_element_type=F32)
        acc = part if acc is None else acc + part
    hn = x + 0.5 * _rms(acc, gout_ref[...])
    hout_ref[...] = hn
    if emit_u:
        rest[0][...] = _rms(hn, gnext_ref[...]).astype(rest[0].dtype)


def _ffn(h, g_in, g_out, g_next, w_in, w_out, layer, half, *, emit_u, tm, tf, u_dtype=F32, mixer=None,
         mixer_args=()):
    n, d = h.shape
    f = w_out.shape[2]
    row = pl.BlockSpec((tm, d), lambda i: (i, 0))
    resident = dict(pipeline_mode=pl.Buffered(1))
    if mixer == "gdn":
        o, main, o_norm, w_mix, g_mix = mixer_args
        dv = o.shape[1]
        mixer_ops = [(o, pl.BlockSpec((tm, dv), lambda i: (i, 0))),
                     (main, pl.BlockSpec((tm, dv), lambda i: (i, GDN_QKV // dv))),
                     _whole(o_norm, 1), _whole(w_mix, 1, **resident), _whole(g_mix, 1)]
    elif mixer == "rwkv":
        y, gate, w_mix, g_mix = mixer_args
        mixer_ops = [(y, row), (gate, row), _whole(w_mix, 1, **resident), _whole(g_mix, 1)]
    else:
        mixer_ops = []
    assert len(mixer_ops) == _MIXER_REFS[mixer]
    ops = mixer_ops + [(h, row), _whole(g_in, 1), _whole(g_out, 1), _whole(g_next, 1),
                       (w_in, pl.BlockSpec((None, None, d, 2 * f), lambda i: (layer, half, 0, 0), **resident)),
                       (w_out, pl.BlockSpec((None, None, f, d), lambda i: (layer, half, 0, 0), **resident))]
    out_shape = [jax.ShapeDtypeStruct((n, d), F32)]
    out_specs = [row]
    if emit_u:
        out_shape.append(jax.ShapeDtypeStruct((n, d), u_dtype))
        out_specs.append(row)
    return pl.pallas_call(
        functools.partial(_ffn_body, mixer=mixer, emit_u=emit_u, tf=tf),
        grid=(n // tm,),
        in_specs=[spec for _, spec in ops],
        out_specs=out_specs,
        out_shape=out_shape,
        compiler_params=_params("parallel"),
        name="ffn_u" if emit_u else "ffn_" + mixer,
    )(*[op for op, _ in ops])


def _gdn_proj_body(u_ref, w_ref, wba_ref, main_ref, ba_ref):
    xb = u_ref[...].astype(BF16)
    main_ref[...] = jnp.dot(xb, w_ref[...], preferred_element_type=F32)
    ba_ref[...] = jnp.dot(xb, wba_ref[...], preferred_element_type=F32)


def _gdn_proj(u, w_main, w_ba, *, tm):
    n, d = u.shape
    nm = w_main.shape[1]
    resident = dict(pipeline_mode=pl.Buffered(1))
    w_main, w_main_spec = _whole(w_main, 1, **resident)
    w_ba, w_ba_spec = _whole(w_ba, 1, **resident)
    return pl.pallas_call(
        _gdn_proj_body,
        grid=(n // tm,),
        in_specs=[pl.BlockSpec((tm, d), lambda i: (i, 0)), w_main_spec, w_ba_spec],
        out_specs=[pl.BlockSpec((tm, nm), lambda i: (i, 0)),
                   pl.BlockSpec((tm, LANES), lambda i: (i, 0))],
        out_shape=[jax.ShapeDtypeStruct((n, nm), F32), jax.ShapeDtypeStruct((n, LANES), F32)],
        compiler_params=_params("parallel"),
        name="gdn_proj",
    )(u, w_main, w_ba)


def _gdn_rec_body(cur_ref, prev_ref, init_ref, ba_ref, cw_ref, alr_ref, dbr_ref,
                  s0_ref, *rest, chunk, t_valid, nb, layer, first_writer):
    o_ref, sout_ref, ext_sc, s_sc, *bd_scratch = rest if first_writer else rest[1:]
    sout = sout_ref.at[layer] if first_writer else sout_ref
    c = pl.program_id(1)
    hd = GDN_HEAD_DIM
    nh = GDN_HEADS
    sgh = MXU_TILE // hd
    nsg = nh // sgh

    @pl.when(c == 0)
    def _():
        s_sc[...] = jnp.zeros_like(s_sc)
        for q in range(nb):
            for h in range(nh):
                g, j = divmod(h, sgh)
                s_sc[q * nsg + g, j * hd:(j + 1) * hd, j * hd:(j + 1) * hd] = s0_ref[q, h]
        for ref in bd_scratch:
            ref[...] = jnp.zeros_like(ref)
        ext_sc[:, 0:SUBLANES, :] = init_ref[...]

    @pl.when(c > 0)
    def _():
        ext_sc[:, 0:SUBLANES, :] = prev_ref[...]

    ext_sc[:, SUBLANES:SUBLANES + chunk, :] = cur_ref[...]
    incl, _ = _tri_incl(chunk)
    tri_bf = incl.astype(BF16)

    kb, qn, kn, vb, kbeg, qdec, kdec, gcol, dlast = ([] for _ in range(9))
    for s_ in range(nb):
        first = SUBLANES - (GDN_CONV - 1)
        conv = cw_ref[0:1, :] * ext_sc[s_, first:first + chunk, :]
        for tap in range(1, GDN_CONV):
            conv = conv + cw_ref[tap:tap + 1, :] * ext_sc[s_, first + tap:first + tap + chunk, :]
        qkv = conv * jax.nn.sigmoid(conv)

        ba = ba_ref[s_]
        g_cols = -jnp.exp(alr_ref[...]) * _softplus(ba + dbr_ref[...])
        beta_cols = jax.nn.sigmoid(ba)
        if t_valid < chunk:
            row_ok = lax.broadcasted_iota(jnp.int32, (chunk, 1), 0) < t_valid
            qkv = jnp.where(row_ok, qkv, 0.0)
            g_cols = jnp.where(row_ok, g_cols, 0.0)
        gc_cols = _cumsum_rows(tri_bf, g_cols)

        for h in range(nh):
            q = qkv[:, h * hd:(h + 1) * hd]
            k = qkv[:, (nh + h) * hd:(nh + h + 1) * hd]
            v = qkv[:, (2 * nh + h) * hd:(2 * nh + h + 1) * hd]
            qn_h = q * lax.rsqrt(jnp.sum(q * q, axis=-1, keepdims=True) + L2_EPS) * (hd ** -0.5)
            kn_h = k * lax.rsqrt(jnp.sum(k * k, axis=-1, keepdims=True) + L2_EPS)
            beta = beta_cols[:, h:h + 1]
            gc = gc_cols[:, nh + h:nh + h + 1]
            eg = jnp.exp(gc)
            g_last = gc[chunk - 1:chunk, :]
            kb_h = kn_h * beta
            kb.append(kb_h)
            qn.append(qn_h)
            kn.append(kn_h)
            vb.append(v * beta)
            kbeg.append(kb_h * eg)
            qdec.append(qn_h * eg)
            kdec.append(kn_h * jnp.exp(g_last - gc))
            gcol.append(gc)
            dlast.append(jnp.exp(g_last))

    hg = min(nh, MXU_TILE // chunk)
    ntg = nh // hg
    wt = hg * chunk
    kt = hg * hd
    sg_per_tg = kt // MXU_TILE
    tgs = range(nb * ntg)
    cat = lambda xs, t: jnp.concatenate(xs[t * hg:(t + 1) * hg], axis=1)
    bdc_ref, bdt_ref = bd_scratch if bd_scratch else (None, None)
    bd_c = _BlockDiag(chunk, hg, hd, bdc_ref)
    bd_t = _BlockDiag(chunk, hg, chunk, bdt_ref)
    state_mask = _block_mask(MXU_TILE, MXU_TILE, hd, hd)
    ri = lax.broadcasted_iota(jnp.int32, (chunk, wt), 0)
    cw = lax.broadcasted_iota(jnp.int32, (chunk, wt), 1)
    cj = cw & (chunk - 1)
    strict_w = ri > cj
    incl_w = ri >= cj
    ones_cc = jnp.ones((chunk, chunk), BF16)

    decay = []
    for t in tgs:
        gce = jnp.broadcast_to(gcol[t * hg], (chunk, wt))
        for j in range(1, hg):
            gce = jnp.where(cw >= j * chunk, jnp.broadcast_to(gcol[t * hg + j], (chunk, wt)), gce)
        hi, mid, lo = _split3(jnp.where(ri == cj, gce, 0.0))
        gcj = (jnp.dot(ones_cc, hi, preferred_element_type=F32)
               + (jnp.dot(ones_cc, mid, preferred_element_type=F32) + jnp.dot(ones_cc, lo, preferred_element_type=F32)))
        decay.append(jnp.exp(jnp.where(incl_w, gce - gcj, -jnp.inf)))

    lhs = [jnp.concatenate([cat(kb, t), cat(qn, t)], axis=0).astype(BF16) for t in tgs]
    mk = [lax.dot_general(lhs[t], bd_c(cat(kn, t)), (((1,), (1,)), ((), ())),
                          preferred_element_type=F32) for t in tgs]
    a_mat = [jnp.where(strict_w, mk[t][:chunk] * decay[t], 0.0) for t in tgs]
    attn = [jnp.where(incl_w, mk[t][chunk:] * decay[t], 0.0) for t in tgs]
    t_mat = _unit_lower_inverse_wide(a_mat, chunk, hg, bd_t)
    t_bf = [x.astype(BF16) for x in t_mat]
    u = [jnp.dot(t_bf[t], bd_c(cat(vb, t)), preferred_element_type=F32) for t in tgs]
    w = [jnp.dot(t_bf[t], bd_c(cat(kbeg, t)), preferred_element_type=F32) for t in tgs]

    v_new, qs = [], []
    for t in tgs:
        qd = cat(qdec, t)
        parts = []
        for q_ in range(sg_per_tg):
            sl = slice(q_ * MXU_TILE, (q_ + 1) * MXU_TILE)
            parts.append(_dot(jnp.concatenate([w[t][:, sl], qd[:, sl]], axis=0), s_sc[t * sg_per_tg + q_]))
        ws = parts[0] if sg_per_tg == 1 else jnp.concatenate(parts, axis=1)
        v_new.append(u[t] - ws[:chunk])
        qs.append(ws[chunk:])
    o = [qs[t] + jnp.dot(attn[t].astype(BF16), bd_c(v_new[t]), preferred_element_type=F32)
         for t in tgs]
    join = lambda xs, s_: xs[s_] if ntg == 1 else jnp.concatenate(xs[s_ * ntg:(s_ + 1) * ntg], axis=1)
    lane = lax.broadcasted_iota(jnp.int32, (1, MXU_TILE), 1)
    for s_ in range(nb):
        o_ref[s_] = join(o, s_)
        vn_all = join(v_new, s_)
        for g in range(nsg):
            sl = slice(g * MXU_TILE, (g + 1) * MXU_TILE)
            h0 = s_ * nh + g * sgh
            dl = jnp.broadcast_to(dlast[h0], (1, MXU_TILE))
            for j in range(1, sgh):
                dl = jnp.where(lane >= j * hd, jnp.broadcast_to(dlast[h0 + j], (1, MXU_TILE)), dl)
            upd = _dot_tn(jnp.concatenate(kdec[h0:h0 + sgh], axis=1), vn_all[:, sl])
            s_sc[s_ * nsg + g] = s_sc[s_ * nsg + g] * dl + jnp.where(state_mask, upd, 0.0)

    @pl.when(c == pl.num_programs(1) - 1)
    def _():
        if first_writer:
            sout_ref[...] = jnp.zeros_like(sout_ref)
        for s_ in range(nb):
            for h in range(nh):
                g, j = divmod(h, sgh)
                sout[s_, h] = s_sc[s_ * nsg + g, j * hd:(j + 1) * hd, j * hd:(j + 1) * hd]


def _gdn_rec(qkv, conv_init, ba, conv_w, alr, dbr, s0, s_acc, layer, *, chunk, t_valid, nb):
    bsz, tp, width = qkv.shape
    assert bsz % nb == 0 and tp % chunk == 0
    blk = (nb, GDN_HEADS, GDN_HEAD_DIM, GDN_HEAD_DIM)
    state = pl.BlockSpec((None,) + blk, lambda b, c: (layer, b, 0, 0, 0))
    first = s_acc is None
    out_state = pl.BlockSpec((s0.shape[0],) + blk, lambda b, c: (0, b, 0, 0, 0)) if first else state
    nc = tp // chunk
    cb = chunk // SUBLANES
    dv = GDN_HEADS * GDN_HEAD_DIM
    (conv_w, conv_w_spec), (alr, alr_spec), (dbr, dbr_spec) = (_whole(x, 2) for x in (conv_w, alr, dbr))
    return pl.pallas_call(
        functools.partial(_gdn_rec_body, chunk=chunk, t_valid=t_valid, nb=nb, layer=layer, first_writer=first),
        grid=(bsz // nb, nc),
        in_specs=[pl.BlockSpec((nb, chunk, GDN_QKV), lambda b, c: (b, c, 0)),
                  pl.BlockSpec((nb, SUBLANES, GDN_QKV), lambda b, c: (b, jnp.maximum(c * cb - 1, 0), 0)),
                  pl.BlockSpec((nb, SUBLANES, GDN_QKV), lambda b, c: (b, 0, 0)),
                  pl.BlockSpec((nb, chunk, LANES), lambda b, c: (b, c, 0)),
                  conv_w_spec, alr_spec, dbr_spec,
                  state] + ([] if first else [pl.BlockSpec(memory_space=pl.ANY)]),
        out_specs=[pl.BlockSpec((nb, chunk, dv), lambda b, c: (b, c, 0)), out_state],
        out_shape=[jax.ShapeDtypeStruct((bsz, tp, dv), F32), jax.ShapeDtypeStruct(s0.shape, F32)],
        input_output_aliases={} if first else {8: 1},
        scratch_shapes=[pltpu.VMEM((nb, chunk + SUBLANES, GDN_QKV), F32),
                        pltpu.VMEM((nb * dv // MXU_TILE, MXU_TILE, MXU_TILE), F32)]
        + _block_diag_scratch(chunk, nb, GDN_HEADS, GDN_HEAD_DIM, channel_sites=4),
        compiler_params=_params("parallel", "arbitrary"),
        name="gdn_rec",
    )(qkv, qkv, conv_init, ba, conv_w, alr, dbr, s0, *([] if first else [s_acc]))


def _rwkv_proj_body(*refs, has_vres, seq_len, tm):
    if has_vres:
        (x_ref, prev_ref, start_ref, vf_ref, lerp_ref, wr_ref, wk_ref, wv_ref, w0_ref, w1_ref, w2_ref, a0_ref,
         a1_ref, a2_ref, g1_ref, g2_ref, v0_ref, v1_ref, v2_ref, r_ref, k_ref, v_ref, lw_ref, a_ref, g_ref) = refs
    else:
        (x_ref, prev_ref, start_ref, lerp_ref, wr_ref, wk_ref, wv_ref, w0_ref, w1_ref, w2_ref, a0_ref, a1_ref,
         a2_ref, g1_ref, g2_ref, r_ref, k_ref, v_ref, lw_ref, a_ref, g_ref) = refs
    x = x_ref[...]
    rolled = pltpu.roll(x, 1, axis=0)
    rows = lax.broadcasted_iota(jnp.int32, (tm, 1), 0)
    if tm <= seq_len:
        i = pl.program_id(0)
        tiles_per_seq = seq_len // tm
        carried = start_ref[pl.ds(i // tiles_per_seq, 1), :]
        before = prev_ref[SUBLANES - 1:SUBLANES, :]
        first = jnp.where(i % tiles_per_seq == 0, carried, before)
        xp = jnp.where(rows == 0, first, rolled)
    else:
        xp = jnp.where((rows & (seq_len - 1)) == 0, start_ref[...], rolled)
    dx = xp - x

    def mix(s):
        return (x + dx * lerp_ref[s:s + 1, :]).astype(BF16)

    r_ref[...] = jnp.dot(mix(0), wr_ref[...], preferred_element_type=F32)
    k_ref[...] = jnp.dot(mix(1), wk_ref[...], preferred_element_type=F32)
    xv = mix(2)
    v = jnp.dot(xv, wv_ref[...], preferred_element_type=F32)
    if has_vres:
        gate = jax.nn.sigmoid(v0_ref[...] + _dot(jnp.dot(xv, v1_ref[...], preferred_element_type=F32), v2_ref[...]))
        v = v + (vf_ref[...] - v) * gate
    v_ref[...] = v
    dec = w0_ref[...] + _dot(jnp.tanh(jnp.dot(mix(3), w1_ref[...], preferred_element_type=F32)), w2_ref[...])
    lw_ref[...] = -math.exp(-0.5) * jax.nn.sigmoid(dec)
    a_ref[...] = jax.nn.sigmoid(a0_ref[...] + _dot(jnp.dot(mix(4), a1_ref[...], preferred_element_type=F32),
                                                   a2_ref[...]))
    g_ref[...] = _dot(jax.nn.sigmoid(jnp.dot(mix(5), g1_ref[...], preferred_element_type=F32)), g2_ref[...])


def _rwkv_proj(x, shift, v_first, lerp, wr, wk, wv, w0, w1, w2, a0, a1, a2, g1, g2, vres, *, seq_len, tm):
    n, d = x.shape
    has_vres = vres is not None
    row = pl.BlockSpec((tm, d), lambda i: (i, 0))
    full = lambda a: _whole(a, 1, pipeline_mode=pl.Buffered(1))
    tb = tm // SUBLANES
    if tm <= seq_len:
        assert seq_len % tm == 0
        start = full(shift)
    else:
        assert tm % seq_len == 0 and seq_len & (seq_len - 1) == 0
        start = (jnp.repeat(shift.arr[shift.idx], seq_len, axis=0), row)
    prev_spec = pl.BlockSpec((SUBLANES, d), lambda i: (jnp.maximum(i * tb - 1, 0), 0))
    ops = ([(x, row), (x, prev_spec), start] + ([(v_first, row)] if has_vres else [])
           + [full(a) for a in (lerp, wr, wk, wv, w0, w1, w2, a0, a1, a2, g1, g2) + (tuple(vres) if has_vres else ())])
    args = [op for op, _ in ops]
    in_specs = [spec for _, spec in ops]
    return pl.pallas_call(
        functools.partial(_rwkv_proj_body, has_vres=has_vres, seq_len=seq_len, tm=tm),
        grid=(n // tm,),
        in_specs=in_specs,
        out_specs=[row] * 6,
        out_shape=[jax.ShapeDtypeStruct((n, d), F32)] * 6,
        compiler_params=_params("parallel"),
        name="rwkv_proj_vres" if has_vres else "rwkv_proj",
    )(*args)


def _rwkv_rec_body(r_ref, k_ref, v_ref, lw_ref, a_ref, kk_ref, ka_ref, rk_ref, lnw_ref, lnb_ref, s0_ref,
                   *rest, chunk, t_valid, heads, nb, layer, first_writer):
    y_ref, sout_ref, s_sc, *bd_scratch = rest if first_writer else rest[1:]
    sout = sout_ref.at[layer] if first_writer else sout_ref
    c = pl.program_id(1)
    hd = RWKV_HEAD
    d = heads * hd
    gw = MXU_TILE
    sgh = gw // hd
    nsg = heads // sgh
    hg = min(heads, gw // chunk)
    ntg = heads // hg
    kt = hg * hd
    sg_per_tg = kt // gw

    @pl.when(c == 0)
    def _():
        s_sc[...] = jnp.zeros_like(s_sc)
        for q in range(nb):
            for h in range(heads):
                g, j = divmod(h, sgh)
                s_sc[q * nsg + g, j * hd:(j + 1) * hd, j * hd:(j + 1) * hd] = s0_ref[q, h]
        for ref in bd_scratch:
            ref[...] = jnp.zeros_like(ref)

    state_mask = _block_mask(gw, gw, hd, hd)
    ones_bd = state_mask.astype(BF16)

    def head_sums(xs):
        parts = []
        for x in xs:
            for g in range(nsg):
                parts.extend(_split2(x[:, g * gw:(g + 1) * gw]))
        sums = _shared_rhs_dots(parts, ones_bd)
        return [jnp.concatenate([sums[2 * (i * nsg + g)] + sums[2 * (i * nsg + g) + 1] for g in range(nsg)], axis=1)
                for i in range(len(xs))]

    incl, _ = _tri_incl(chunk)
    tri_bf = incl.astype(BF16)

    def load(q):
        r_all = r_ref[q]
        k_all = k_ref[q]
        v_all = v_ref[q]
        lw_all = lw_ref[q]
        a_all = a_ref[q]
        if t_valid < chunk:
            row_ok = lax.broadcasted_iota(jnp.int32, (chunk, 1), 0) < t_valid
            r_all = jnp.where(row_ok, r_all, 0.0)
            k_all = jnp.where(row_ok, k_all, 0.0)
            v_all = jnp.where(row_ok, v_all, 0.0)
            lw_all = jnp.where(row_ok, lw_all, 0.0)
        kx = k_all * kk_ref[...]
        kmod = k_all * (1.0 + (a_all - 1.0) * ka_ref[...])
        return dict(r=r_all, v=v_all, lw=lw_all, a=a_all, kx=kx, kmod=kmod)

    seqs = [load(q) for q in range(nb)]
    sums = head_sums([s["kx"] * s["kx"] for s in seqs] + [s["r"] * s["kmod"] * rk_ref[...] for s in seqs])

    def prep(q):
        s = seqs[q]
        gc = _cumsum_rows(tri_bf, s["lw"])
        kk = s["kx"] * lax.rsqrt(sums[q] + L2_EPS)
        bv = kk * s["a"]
        e_neg = jnp.exp(-gc)
        g_last = gc[chunk - 1:chunk, :]
        e_rem = jnp.exp(g_last - gc)
        return dict(v=s["v"], rk_sum=sums[nb + q], at=-kk * jnp.exp(gc - s["lw"]), rt=s["r"] * jnp.exp(gc),
                    bt=bv * e_neg, kt=s["kmod"] * e_neg, bp=bv * e_rem, kp=s["kmod"] * e_rem, dec=jnp.exp(g_last))

    seqs = [prep(q) for q in range(nb)]

    bdc_ref, bdt_ref = bd_scratch if bd_scratch else (None, None)
    bd_c = _BlockDiag(chunk, hg, hd, bdc_ref)
    bd_t = _BlockDiag(chunk, hg, chunk, bdt_ref)
    ri = lax.broadcasted_iota(jnp.int32, (chunk, hg * chunk), 0)
    cj = lax.broadcasted_iota(jnp.int32, (chunk, hg * chunk), 1) & (chunk - 1)
    strict_w = ri > cj
    incl_w = ri >= cj

    tgs = range(nb * ntg)
    lane = lambda name, t: seqs[t // ntg][name][:, (t % ntg) * kt:(t % ntg + 1) * kt]
    lhs = [jnp.concatenate([lane("at", t), lane("rt", t)], axis=0).astype(BF16) for t in tgs]
    mb = [lax.dot_general(lhs[t], bd_c(lane("bt", t)), (((1,), (1,)), ((), ())),
                          preferred_element_type=F32) for t in tgs]
    mk = [lax.dot_general(lhs[t], bd_c(lane("kt", t)), (((1,), (1,)), ((), ())),
                          preferred_element_type=F32) for t in tgs]
    a_ab = [jnp.where(strict_w, mb[t][:chunk], 0.0) for t in tgs]
    a_rb = [jnp.where(incl_w, mb[t][chunk:], 0.0) for t in tgs]
    a_ak = [jnp.where(strict_w, mk[t][:chunk], 0.0) for t in tgs]
    a_rk = [jnp.where(incl_w, mk[t][chunk:], 0.0) for t in tgs]
    t_mat = _unit_lower_inverse_wide([-x for x in a_ab], chunk, hg, bd_t)

    xs = []
    for t in tgs:
        parts = []
        for q in range(sg_per_tg):
            g = t * sg_per_tg + q
            parts.append(lax.dot_general(lhs[t][:, q * gw:(q + 1) * gw], s_sc[g].astype(BF16),
                                         (((1,), (1,)), ((), ())), preferred_element_type=F32))
        xs.append(parts[0] if len(parts) == 1 else jnp.concatenate(parts, axis=1))
    av = [_shared_rhs_dots([a_ak[t], a_rk[t]], bd_c(lane("v", t))) for t in tgs]
    rhs_u = [xs[t][:chunk] + av[t][0] for t in tgs]
    u = [jnp.dot(t_mat[t].astype(BF16), bd_c(rhs_u[t]), preferred_element_type=F32) for t in tgs]
    y = [xs[t][chunk:] + jnp.dot(a_rb[t].astype(BF16), bd_c(u[t]), preferred_element_type=F32) + av[t][1]
         for t in tgs]

    join = lambda xs_, q: xs_[q] if ntg == 1 else jnp.concatenate(xs_[q * ntg:(q + 1) * ntg], axis=1)
    for q in range(nb):
        sq = seqs[q]
        u_all = join(u, q)
        for g in range(nsg):
            sl = slice(g * gw, (g + 1) * gw)
            upd = _dot_tn(jnp.concatenate([u_all[:, sl], sq["v"][:, sl]], axis=0),
                          jnp.concatenate([sq["bp"][:, sl], sq["kp"][:, sl]], axis=0))
            s_sc[q * nsg + g] = s_sc[q * nsg + g] * sq["dec"][:, sl] + jnp.where(state_mask, upd, 0.0)

    y_all = [join(y, q) for q in range(nb)]
    mean = head_sums(y_all)
    yc = [y_all[q] - mean[q] * (1.0 / hd) for q in range(nb)]
    var = head_sums([x * x for x in yc])
    for q in range(nb):
        yn = yc[q] * lax.rsqrt(var[q] * (1.0 / hd) + GN_EPS)
        y_ref[q] = yn * lnw_ref[...] + lnb_ref[...] + seqs[q]["rk_sum"] * seqs[q]["v"]

    @pl.when(c == pl.num_programs(1) - 1)
    def _():
        if first_writer:
            sout_ref[...] = jnp.zeros_like(sout_ref)
        for q in range(nb):
            for h in range(heads):
                g, j = divmod(h, sgh)
                sout[q, h] = s_sc[q * nsg + g, j * hd:(j + 1) * hd, j * hd:(j + 1) * hd]


def _rwkv_rec(r, k, v, lw, a, k_k, k_a, r_k, ln_w, ln_b, s0, s_acc, layer, *, chunk, t_valid, nb):
    bsz, tp, d = r.shape
    assert bsz % nb == 0 and tp % chunk == 0
    heads = d // RWKV_HEAD
    nc = tp // chunk
    seq = pl.BlockSpec((nb, chunk, d), lambda b, c: (b, c, 0))
    vecs = [_whole(x, 2) for x in (k_k, k_a, r_k, ln_w, ln_b)]
    blk = (nb, heads, RWKV_HEAD, RWKV_HEAD)
    st = pl.BlockSpec((None,) + blk, lambda b, c: (layer, b, 0, 0, 0))
    first = s_acc is None
    out_st = pl.BlockSpec((s0.shape[0],) + blk, lambda b, c: (0, b, 0, 0, 0)) if first else st
    return pl.pallas_call(
        functools.partial(_rwkv_rec_body, chunk=chunk, t_valid=t_valid, heads=heads, nb=nb, layer=layer,
                          first_writer=first),
        grid=(bsz // nb, nc),
        in_specs=[seq] * 5 + [spec for _, spec in vecs] + [st] + ([] if first else [pl.BlockSpec(memory_space=pl.ANY)]),
        out_specs=[seq, out_st],
        out_shape=[jax.ShapeDtypeStruct((bsz, tp, d), F32), jax.ShapeDtypeStruct(s0.shape, F32)],
        input_output_aliases={} if first else {11: 1},
        scratch_shapes=[pltpu.VMEM((nb * d // MXU_TILE, MXU_TILE, MXU_TILE), F32)]
        + _block_diag_scratch(chunk, nb, heads, RWKV_HEAD, channel_sites=5),
        compiler_params=_params("parallel", "arbitrary"),
        name="rwkv_rec",
    )(r, k, v, lw, a, *[op for op, _ in vecs], s0, *([] if first else [s_acc]))


def _pad_time(x, tp):
    t = x.shape[1]
    if t == tp:
        return x
    return jnp.pad(x, [(0, 0), (0, tp - t)] + [(0, 0)] * (x.ndim - 2))


def _tiles(n):
    return 1024 if n % 1024 == 0 else n


def _trunk(h, bsz, t, gdn_s, gdn_buf, rw_s, rw_shift, p, chunk, nb):
    n, d = h.shape
    depth = p["norm_g"].shape[0]
    tm = _tiles(n)
    tp = -(-t // chunk) * chunk
    out_gdn_buf, out_rw_shift = [], []
    out_gdn_s = out_rw_s = None
    v_first = None
    for i in range(depth):
        row = lambda s: _Layer(p["norm_g"], i, s)
        h, u = _ffn(h, row(0), row(1), row(2), p["ffn_w_in"], p["ffn_w_out"], i, 0,
                    emit_u=True, tm=min(tm, 512), tf=256, u_dtype=BF16 if i % 2 == 0 else F32)
        j = i // 2
        if i % 2 == 0:
            at = lambda name: _Layer(p[name], j)
            main, ba = _gdn_proj(u, at("gdn_w_main"), at("gdn_w_ba"), tm=min(tm, 512))
            main3 = main.reshape(bsz, t, main.shape[1])
            out_gdn_buf.append(main3[:, t - (GDN_CONV - 1):, :GDN_QKV])
            ba3 = _pad_time(ba.reshape(bsz, t, LANES), tp)
            conv_init = jnp.pad(gdn_buf[j], [(0, 0), (SUBLANES - (GDN_CONV - 1), 0), (0, 0)])
            qkv = main3 if tp == t else _pad_time(main3[:, :, :GDN_QKV], tp)
            o, out_gdn_s = _gdn_rec(qkv, conv_init, ba3, at("gdn_conv_w"), at("gdn_alr"), at("gdn_dbr"),
                                    gdn_s, out_gdn_s, j, chunk=chunk, t_valid=min(t, chunk), nb=nb)
            o2 = o[:, :t].reshape(n, o.shape[2])
            mixer, mixer_args = "gdn", (o2, main, at("gdn_o_norm"), at("gdn_w_out"), row(3))
        else:
            u3 = u.reshape(bsz, t, d)
            out_rw_shift.append(u3[:, -1])
            at = lambda name: _Layer(p["rwkv_" + name], j)
            vres = None if j == 0 else tuple(_Layer(p["rwkv_" + name], j - 1) for name in ("v0", "v1", "v2"))
            w_rkv = [_Layer(p["rwkv_w_rkv"], j, s) for s in range(3)]
            r, k, v, lw, a, g = _rwkv_proj(
                u, _Layer(rw_shift, j), v_first, at("lerp"), *w_rkv, at("w0"), at("w1"), at("w2"), at("a0"),
                at("a1"), at("a2"), at("g1"), at("g2"), vres, seq_len=t, tm=min(tm, 512))
            if v_first is None:
                v_first = v
            seq = lambda x: _pad_time(x.reshape(bsz, t, d), tp)
            y, out_rw_s = _rwkv_rec(seq(r), seq(k), seq(v), seq(lw), seq(a), at("k_k"), at("k_a"),
                                    at("r_k"), at("ln_w"), at("ln_b"), rw_s, out_rw_s, j,
                                    chunk=chunk, t_valid=min(t, chunk), nb=nb)
            mixer, mixer_args = "rwkv", (y[:, :t].reshape(n, d), g, at("w_o"), row(3))
        h = _ffn(h, row(4), row(5), row(5), p["ffn_w_in"], p["ffn_w_out"], i, 1,
                 emit_u=False, tm=min(tm, 512), tf=256, mixer=mixer, mixer_args=mixer_args)[0]
    return h.reshape(bsz, t, d), out_gdn_s, jnp.stack(out_gdn_buf), out_rw_s, jnp.stack(out_rw_shift)


def kernel(x_prompt, x_sample, state_gdn, state_gdn_conv, state_rwkv, state_rwkv_shift, norm_g, ffn_w_in, ffn_w_out, gdn_w_in, gdn_conv_w, gdn_a_log, gdn_dt_bias, gdn_o_norm, gdn_w_out, rwkv_lerp, rwkv_w_rkv, rwkv_w0, rwkv_w1, rwkv_w2, rwkv_a0, rwkv_a1, rwkv_a2, rwkv_v0, rwkv_v1, rwkv_v2, rwkv_g1, rwkv_g2, rwkv_k_k, rwkv_k_a, rwkv_r_k, rwkv_ln_w, rwkv_ln_b, rwkv_w_o):
    bsz, t, d = x_prompt.shape
    dbsz, dt, _ = x_sample.shape
    n_gdn = gdn_w_in.shape[0]
    n_rwkv = rwkv_w_rkv.shape[0]
    nh = GDN_HEADS
    n_main = GDN_QKV + nh * GDN_HEAD_DIM

    def lane_slot(x, offset):
        return jnp.pad(x, [(0, 0), (offset, LANES - offset - nh)])[:, None, :]

    row3 = lambda x: x[:, None, :]
    p = dict(
        norm_g=norm_g[:, :, None, :],
        ffn_w_in=ffn_w_in.astype(BF16),
        ffn_w_out=ffn_w_out.astype(BF16),
        gdn_w_main=gdn_w_in[:, :, :n_main].astype(BF16),
        gdn_w_ba=jnp.pad(gdn_w_in[:, :, n_main:], [(0, 0), (0, 0), (0, LANES - 2 * nh)]).astype(BF16),
        gdn_conv_w=gdn_conv_w,
        gdn_alr=lane_slot(gdn_a_log, nh), gdn_dbr=lane_slot(gdn_dt_bias, nh),
        gdn_o_norm=row3(gdn_o_norm),
        gdn_w_out=gdn_w_out.astype(BF16),
        rwkv_lerp=jnp.pad(rwkv_lerp, [(0, 0), (0, SUBLANES - rwkv_lerp.shape[1]), (0, 0)]),
        rwkv_w_rkv=rwkv_w_rkv.astype(BF16),
        rwkv_w0=row3(rwkv_w0), rwkv_w1=rwkv_w1.astype(BF16), rwkv_w2=rwkv_w2.astype(BF16),
        rwkv_a0=row3(rwkv_a0), rwkv_a1=rwkv_a1.astype(BF16), rwkv_a2=rwkv_a2.astype(BF16),
        rwkv_v0=row3(rwkv_v0), rwkv_v1=rwkv_v1.astype(BF16), rwkv_v2=rwkv_v2.astype(BF16),
        rwkv_g1=rwkv_g1.astype(BF16), rwkv_g2=rwkv_g2.astype(BF16),
        rwkv_k_k=row3(rwkv_k_k), rwkv_k_a=row3(rwkv_k_a),
        rwkv_r_k=rwkv_r_k.reshape(n_rwkv, 1, d),
        rwkv_ln_w=row3(rwkv_ln_w), rwkv_ln_b=row3(rwkv_ln_b),
        rwkv_w_o=rwkv_w_o.astype(BF16),
    )
    dtype = x_prompt.dtype
    zero_gdn = jnp.zeros((n_gdn, bsz) + state_gdn.shape[2:], dtype)
    zero_conv = jnp.zeros((n_gdn, bsz) + state_gdn_conv.shape[2:], dtype)
    zero_rwkv = jnp.zeros((n_rwkv, bsz) + state_rwkv.shape[2:], dtype)
    zero_shift = jnp.zeros((n_rwkv, bsz, d), dtype)
    prompt = _trunk(x_prompt.reshape(bsz * t, d), bsz, t, zero_gdn, zero_conv, zero_rwkv, zero_shift, p,
                    chunk=64, nb=2)
    sample = _trunk(x_sample.reshape(dbsz * dt, d), dbsz, dt, state_gdn, state_gdn_conv, state_rwkv,
                    state_rwkv_shift, p, chunk=SUBLANES, nb=8)
    return (prompt[0], sample[0]) + prompt[1:] + sample[1:]
```

```python
import functools
import math

import jax
import jax.numpy as jnp
from jax import lax
from jax.experimental import pallas as pl
from jax.experimental.pallas import tpu as pltpu

F32 = jnp.float32
BF16 = jnp.bfloat16

NORM_EPS = 1e-6
L2_EPS = 1e-6
GN_EPS = 64e-5

GDN_HEADS = 8
GDN_HEAD_DIM = 128
GDN_QKV = 3 * GDN_HEADS * GDN_HEAD_DIM
GDN_CONV = 4
RWKV_HEAD = 64

V7X_VMEM_BYTES = 64 * 1024 * 1024
VMEM_LIMIT_BYTES = V7X_VMEM_BYTES * 7 // 8
SUBLANES = 8
LANES = 128
BF16_ROWS = 16
MXU_TILE = 256


def _params(*semantics):
    return pltpu.CompilerParams(dimension_semantics=semantics, vmem_limit_bytes=VMEM_LIMIT_BYTES)


class _Layer:
    def __init__(self, arr, *idx):
        self.arr, self.idx = arr, idx

    @property
    def shape(self):
        return self.arr.shape[len(self.idx):]


def _whole(x, grid_rank, **spec_kwargs):
    arr, idx = (x.arr, x.idx) if isinstance(x, _Layer) else (x, ())
    index = idx + (0,) * (arr.ndim - len(idx))
    index_map = (lambda i: index) if grid_rank == 1 else (lambda i, j: index)
    return arr, pl.BlockSpec((None,) * len(idx) + arr.shape[len(idx):], index_map, **spec_kwargs)


def _rms(x, g):
    return x * lax.rsqrt(jnp.mean(x * x, axis=-1, keepdims=True) + NORM_EPS) * g


def _softplus(x):
    return jnp.maximum(x, 0.0) + jnp.log1p(jnp.exp(-jnp.abs(x)))


def _dot(a, b):
    return jnp.dot(a.astype(BF16), b.astype(BF16), preferred_element_type=F32)


def _dot_tn(a, b):
    return lax.dot_general(a.astype(BF16), b.astype(BF16), (((0,), (0,)), ((), ())),
                           preferred_element_type=F32)


def _split2(a):
    hi = a.astype(BF16)
    lo = (a - hi.astype(F32)).astype(BF16)
    return hi, lo


def _split3(a):
    hi = a.astype(BF16)
    r = a - hi.astype(F32)
    mid = r.astype(BF16)
    lo = (r - mid.astype(F32)).astype(BF16)
    return hi, mid, lo


def _tri_incl(c):
    ri = lax.broadcasted_iota(jnp.int32, (c, c), 0)
    ci = lax.broadcasted_iota(jnp.int32, (c, c), 1)
    return ri >= ci, ri > ci


def _cumsum_rows(tri_bf, x):
    hi, mid, lo = _split3(x)
    return (jnp.dot(tri_bf, hi, preferred_element_type=F32)
            + (jnp.dot(tri_bf, mid, preferred_element_type=F32) + jnp.dot(tri_bf, lo, preferred_element_type=F32)))


def _log2(n):
    assert n & (n - 1) == 0, n
    return n.bit_length() - 1


def _block_mask(rows, cols, rblk, cblk):
    ri = lax.broadcasted_iota(jnp.int32, (rows, cols), 0)
    ci = lax.broadcasted_iota(jnp.int32, (rows, cols), 1)
    return (ri >> _log2(rblk)) == (ci >> _log2(cblk))


def _tile_rows(x, n):
    return x if n == 1 else jnp.concatenate([x] * n, axis=0)


class _BlockDiag:
    def __init__(self, c, n, cb, units, ref=None):
        self.c, self.n, self.cb, self.units, self.ref = c, n, cb, units, ref
        self.mask = None if ref is not None else _block_mask(n * c, n * cb, c, cb)

    def __call__(self, x, site, unit):
        c, n, cb = self.c, self.n, self.cb
        if self.ref is None:
            return jnp.where(self.mask, _tile_rows(x.astype(F32), n), 0.0).astype(BF16)
        slot = site * self.units + unit
        xb = x.astype(BF16)
        for h in range(n):
            self.ref[slot, h * c:(h + 1) * c, h * cb:(h + 1) * cb] = xb[:, h * cb:(h + 1) * cb]
        return self.ref[slot]


def _shared_rhs_dots(xs, rhs):
    m = xs[0].shape[0]
    if len(xs) == 1 or m % BF16_ROWS != 0:
        return [jnp.dot(x.astype(BF16), rhs, preferred_element_type=F32) for x in xs]
    t = jnp.dot(jnp.concatenate([x.astype(BF16) for x in xs], axis=0), rhs, preferred_element_type=F32)
    return [t[i * m:(i + 1) * m] for i in range(len(xs))]


INVERSE_SITES = 2
CHANNEL_SITES = 3


def _unit_lower_inverse_wide(mats, c, n, bd):
    ri = lax.broadcasted_iota(jnp.int32, (c, n * c), 0)
    cj = lax.broadcasted_iota(jnp.int32, (c, n * c), 1) & (c - 1)
    eye = (ri == cj).astype(F32)
    base = (ri >> 3) == (cj >> 3)
    idx = range(len(mats))
    nm = [jnp.where(base, -a, 0.0) for a in mats]
    p = [eye + x for x in nm]
    n2 = [_shared_rhs_dots([nm[i]], bd(nm[i], 1, i))[0] for i in idx]
    pn = [_shared_rhs_dots([p[i], n2[i]], bd(n2[i], 1, i)) for i in idx]
    p = [p[i] + pn[i][0] for i in idx]
    p = [p[i] + _shared_rhs_dots([p[i]], bd(pn[i][1], 1, i))[0] for i in idx]
    if c > SUBLANES:
        a_bd = [bd(mats[i], 0, i) for i in idx]
    s = SUBLANES
    while s < c:
        pa = [_shared_rhs_dots([p[i]], a_bd[i])[0] for i in idx]
        e = [_shared_rhs_dots([pa[i]], bd(p[i], 1, i))[0] for i in idx]
        shift = _log2(2 * s)
        merge = ((ri >> shift) == (cj >> shift)) & ((ri & s) != 0) & ((cj & s) == 0)
        p = [p[i] - jnp.where(merge, e[i], 0.0) for i in idx]
        s *= 2
    return p


def _block_diag_scratch(chunk, nb, heads, head_dim):
    if chunk % BF16_ROWS != 0:
        return []
    hg = min(heads, MXU_TILE // chunk)
    units = nb * (heads // hg)
    return [pltpu.VMEM((CHANNEL_SITES * units, hg * chunk, hg * head_dim), BF16),
            pltpu.VMEM((INVERSE_SITES * units, hg * chunk, hg * chunk), BF16)]


def _gdn_mixer_out(o_ref, z_ref, on_ref, w_ref):
    hd = GDN_HEAD_DIM
    parts = []
    for h in range(GDN_HEADS):
        sl = slice(h * hd, (h + 1) * hd)
        o = o_ref[:, sl]
        z = z_ref[:, sl]
        on = o * lax.rsqrt(jnp.mean(o * o, axis=-1, keepdims=True) + NORM_EPS) * on_ref[...]
        parts.append((on * (z * jax.nn.sigmoid(z))).astype(BF16))
    return jnp.dot(jnp.concatenate(parts, axis=1), w_ref[...], preferred_element_type=F32)


def _rwkv_mixer_out(y_ref, gate_ref, w_ref):
    return jnp.dot((y_ref[...] * gate_ref[...]).astype(BF16), w_ref[...], preferred_element_type=F32)


_MIXER_REFS = {None: 0, "gdn": 5, "rwkv": 4}


def _ffn_body(*refs, mixer, emit_u, tf):
    n_mix = _MIXER_REFS[mixer]
    mix, (h_ref, gin_ref, gout_ref, gnext_ref, wi_ref, wo_ref, hout_ref, *rest) = refs[:n_mix], refs[n_mix:]
    f = wo_ref.shape[0]
    x = h_ref[...]
    if mixer == "gdn":
        x = x + _rms(_gdn_mixer_out(*mix[:4]), mix[4][...])
    elif mixer == "rwkv":
        x = x + _rms(_rwkv_mixer_out(*mix[:3]), mix[3][...])
    xn = _rms(x, gin_ref[...]).astype(BF16)
    acc = None
    for c0 in range(0, f, tf):
        gate = jnp.dot(xn, wi_ref[:, c0:c0 + tf], preferred_element_type=F32)
        up = jnp.dot(xn, wi_ref[:, f + c0:f + c0 + tf], preferred_element_type=F32)
        act = ((gate * jax.nn.sigmoid(gate)) * up).astype(BF16)
        part = jnp.dot(act, wo_ref[c0:c0 + tf, :], preferred_element_type=F32)
        acc = part if acc is None else acc + part
    hn = x + 0.5 * _rms(acc, gout_ref[...])
    hout_ref[...] = hn
    if emit_u:
        rest[0][...] = _rms(hn, gnext_ref[...]).astype(rest[0].dtype)


def _ffn(h, g_in, g_out, g_next, w_in, w_out, layer, half, *, emit_u, tm, tf, u_dtype=F32, mixer=None,
         mixer_args=()):
    n, d = h.shape
    f = w_out.shape[2]
    row = pl.BlockSpec((tm, d), lambda i: (i, 0))
    resident = dict(pipeline_mode=pl.Buffered(1))
    if mixer == "gdn":
        o, main, o_norm, w_mix, g_mix = mixer_args
        dv = o.shape[1]
        mixer_ops = [(o, pl.BlockSpec((tm, dv), lambda i: (i, 0))),
                     (main, pl.BlockSpec((tm, dv), lambda i: (i, GDN_QKV // dv))),
                     _whole(o_norm, 1), _whole(w_mix, 1, **resident), _whole(g_mix, 1)]
    elif mixer == "rwkv":
        y, gate, w_mix, g_mix = mixer_args
        mixer_ops = [(y, row), (gate, row), _whole(w_mix, 1, **resident), _whole(g_mix, 1)]
    else:
        mixer_ops = []
    assert len(mixer_ops) == _MIXER_REFS[mixer]
    ops = mixer_ops + [(h, row), _whole(g_in, 1), _whole(g_out, 1), _whole(g_next, 1),
                       (w_in, pl.BlockSpec((None, None, d, 2 * f), lambda i: (layer, half, 0, 0), **resident)),
                       (w_out, pl.BlockSpec((None, None, f, d), lambda i: (layer, half, 0, 0), **resident))]
    out_shape = [jax.ShapeDtypeStruct((n, d), F32)]
    out_specs = [row]
    if emit_u:
        out_shape.append(jax.ShapeDtypeStruct((n, d), u_dtype))
        out_specs.append(row)
    return pl.pallas_call(
        functools.partial(_ffn_body, mixer=mixer, emit_u=emit_u, tf=tf),
        grid=(n // tm,),
        in_specs=[spec for _, spec in ops],
        out_specs=out_specs,
        out_shape=out_shape,
        compiler_params=_params("parallel"),
        name="ffn_u" if emit_u else "ffn_" + mixer,
    )(*[op for op, _ in ops])


def _gdn_proj_body(u_ref, w_ref, wba_ref, main_ref, ba_ref):
    xb = u_ref[...].astype(BF16)
    main_ref[...] = jnp.dot(xb, w_ref[...], preferred_element_type=F32)
    ba_ref[...] = jnp.dot(xb, wba_ref[...], preferred_element_type=F32)


def _gdn_proj(u, w_main, w_ba, *, tm):
    n, d = u.shape
    nm = w_main.shape[1]
    resident = dict(pipeline_mode=pl.Buffered(1))
    w_main, w_main_spec = _whole(w_main, 1, **resident)
    w_ba, w_ba_spec = _whole(w_ba, 1, **resident)
    return pl.pallas_call(
        _gdn_proj_body,
        grid=(n // tm,),
        in_specs=[pl.BlockSpec((tm, d), lambda i: (i, 0)), w_main_spec, w_ba_spec],
        out_specs=[pl.BlockSpec((tm, nm), lambda i: (i, 0)),
                   pl.BlockSpec((tm, LANES), lambda i: (i, 0))],
        out_shape=[jax.ShapeDtypeStruct((n, nm), F32), jax.ShapeDtypeStruct((n, LANES), F32)],
        compiler_params=_params("parallel"),
        name="gdn_proj",
    )(u, w_main, w_ba)


def _gdn_rec_body(cur_ref, prev_ref, init_ref, ba_ref, cw_ref, alr_ref, dbr_ref,
                  s0_ref, *rest, chunk, t_valid, nb, layer, first_writer):
    o_ref, sout_ref, ext_sc, s_sc, *bd_scratch = rest if first_writer else rest[1:]
    sout = sout_ref.at[layer] if first_writer else sout_ref
    c = pl.program_id(1)
    hd = GDN_HEAD_DIM
    nh = GDN_HEADS
    sgh = MXU_TILE // hd
    nsg = nh // sgh

    @pl.when(c == 0)
    def _():
        s_sc[...] = jnp.zeros_like(s_sc)
        for q in range(nb):
            for h in range(nh):
                g, j = divmod(h, sgh)
                s_sc[q * nsg + g, j * hd:(j + 1) * hd, j * hd:(j + 1) * hd] = s0_ref[q, h]
        for ref in bd_scratch:
            ref[...] = jnp.zeros_like(ref)
        ext_sc[:, 0:SUBLANES, :] = init_ref[...]

    @pl.when(c > 0)
    def _():
        ext_sc[:, 0:SUBLANES, :] = prev_ref[...]

    ext_sc[:, SUBLANES:SUBLANES + chunk, :] = cur_ref[...]
    incl, _ = _tri_incl(chunk)
    tri_bf = incl.astype(BF16)

    kb, qn, kn, vb, kbeg, qdec, kdec, gcol, dlast = ([] for _ in range(9))
    for s_ in range(nb):
        first = SUBLANES - (GDN_CONV - 1)
        conv = cw_ref[0:1, :] * ext_sc[s_, first:first + chunk, :]
        for tap in range(1, GDN_CONV):
            conv = conv + cw_ref[tap:tap + 1, :] * ext_sc[s_, first + tap:first + tap + chunk, :]
        qkv = conv * jax.nn.sigmoid(conv)

        ba = ba_ref[s_]
        g_cols = -jnp.exp(alr_ref[...]) * _softplus(ba + dbr_ref[...])
        beta_cols = jax.nn.sigmoid(ba)
        if t_valid < chunk:
            row_ok = lax.broadcasted_iota(jnp.int32, (chunk, 1), 0) < t_valid
            qkv = jnp.where(row_ok, qkv, 0.0)
            g_cols = jnp.where(row_ok, g_cols, 0.0)
        gc_cols = _cumsum_rows(tri_bf, g_cols)

        for h in range(nh):
            q = qkv[:, h * hd:(h + 1) * hd]
            k = qkv[:, (nh + h) * hd:(nh + h + 1) * hd]
            v = qkv[:, (2 * nh + h) * hd:(2 * nh + h + 1) * hd]
            qn_h = q * lax.rsqrt(jnp.sum(q * q, axis=-1, keepdims=True) + L2_EPS) * (hd ** -0.5)
            kn_h = k * lax.rsqrt(jnp.sum(k * k, axis=-1, keepdims=True) + L2_EPS)
            beta = beta_cols[:, h:h + 1]
            gc = gc_cols[:, nh + h:nh + h + 1]
            eg = jnp.exp(gc)
            g_last = gc[chunk - 1:chunk, :]
            kb_h = kn_h * beta
            kb.append(kb_h)
            qn.append(qn_h)
            kn.append(kn_h)
            vb.append(v * beta)
            kbeg.append(kb_h * eg)
            qdec.append(qn_h * eg)
            kdec.append(kn_h * jnp.exp(g_last - gc))
            gcol.append(gc)
            dlast.append(jnp.exp(g_last))

    hg = min(nh, MXU_TILE // chunk)
    ntg = nh // hg
    wt = hg * chunk
    kt = hg * hd
    sg_per_tg = kt // MXU_TILE
    tgs = range(nb * ntg)
    cat = lambda xs, t: jnp.concatenate(xs[t * hg:(t + 1) * hg], axis=1)
    bdc_ref, bdt_ref = bd_scratch if bd_scratch else (None, None)
    bd_c = _BlockDiag(chunk, hg, hd, nb * ntg, bdc_ref)
    bd_t = _BlockDiag(chunk, hg, chunk, nb * ntg, bdt_ref)
    state_mask = _block_mask(MXU_TILE, MXU_TILE, hd, hd)
    ri = lax.broadcasted_iota(jnp.int32, (chunk, wt), 0)
    cw = lax.broadcasted_iota(jnp.int32, (chunk, wt), 1)
    cj = cw & (chunk - 1)
    strict_w = ri > cj
    incl_w = ri >= cj
    ones_cc = jnp.ones((chunk, chunk), BF16)

    decay = []
    for t in tgs:
        gce = jnp.broadcast_to(gcol[t * hg], (chunk, wt))
        for j in range(1, hg):
            gce = jnp.where(cw >= j * chunk, jnp.broadcast_to(gcol[t * hg + j], (chunk, wt)), gce)
        hi, mid, lo = _split3(jnp.where(ri == cj, gce, 0.0))
        gcj = (jnp.dot(ones_cc, hi, preferred_element_type=F32)
               + (jnp.dot(ones_cc, mid, preferred_element_type=F32) + jnp.dot(ones_cc, lo, preferred_element_type=F32)))
        decay.append(jnp.exp(jnp.where(incl_w, gce - gcj, -jnp.inf)))

    lhs = [jnp.concatenate([cat(kb, t), cat(qn, t)], axis=0).astype(BF16) for t in tgs]
    mk = [lax.dot_general(lhs[t], bd_c(cat(kn, t), 0, t), (((1,), (1,)), ((), ())),
                          preferred_element_type=F32) for t in tgs]
    a_mat = [jnp.where(strict_w, mk[t][:chunk] * decay[t], 0.0) for t in tgs]
    attn = [jnp.where(incl_w, mk[t][chunk:] * decay[t], 0.0) for t in tgs]
    t_mat = _unit_lower_inverse_wide(a_mat, chunk, hg, bd_t)
    t_bf = [x.astype(BF16) for x in t_mat]
    u = [jnp.dot(t_bf[t], bd_c(cat(vb, t), 1, t), preferred_element_type=F32) for t in tgs]
    w = [jnp.dot(t_bf[t], bd_c(cat(kbeg, t), 2, t), preferred_element_type=F32) for t in tgs]

    v_new, qs = [], []
    for t in tgs:
        qd = cat(qdec, t)
        parts = []
        for q_ in range(sg_per_tg):
            sl = slice(q_ * MXU_TILE, (q_ + 1) * MXU_TILE)
            parts.append(_dot(jnp.concatenate([w[t][:, sl], qd[:, sl]], axis=0), s_sc[t * sg_per_tg + q_]))
        ws = parts[0] if sg_per_tg == 1 else jnp.concatenate(parts, axis=1)
        v_new.append(u[t] - ws[:chunk])
        qs.append(ws[chunk:])
    o = [qs[t] + jnp.dot(attn[t].astype(BF16), bd_c(v_new[t], 0, t), preferred_element_type=F32)
         for t in tgs]
    join = lambda xs, s_: xs[s_] if ntg == 1 else jnp.concatenate(xs[s_ * ntg:(s_ + 1) * ntg], axis=1)
    lane = lax.broadcasted_iota(jnp.int32, (1, MXU_TILE), 1)
    for s_ in range(nb):
        o_ref[s_] = join(o, s_)
        vn_all = join(v_new, s_)
        for g in range(nsg):
            sl = slice(g * MXU_TILE, (g + 1) * MXU_TILE)
            h0 = s_ * nh + g * sgh
            dl = jnp.broadcast_to(dlast[h0], (1, MXU_TILE))
            for j in range(1, sgh):
                dl = jnp.where(lane >= j * hd, jnp.broadcast_to(dlast[h0 + j], (1, MXU_TILE)), dl)
            upd = _dot_tn(jnp.concatenate(kdec[h0:h0 + sgh], axis=1), vn_all[:, sl])
            s_sc[s_ * nsg + g] = s_sc[s_ * nsg + g] * dl + jnp.where(state_mask, upd, 0.0)

    @pl.when(c == pl.num_programs(1) - 1)
    def _():
        if first_writer:
            sout_ref[...] = jnp.zeros_like(sout_ref)
        for s_ in range(nb):
            for h in range(nh):
                g, j = divmod(h, sgh)
                sout[s_, h] = s_sc[s_ * nsg + g, j * hd:(j + 1) * hd, j * hd:(j + 1) * hd]


def _gdn_rec(qkv, conv_init, ba, conv_w, alr, dbr, s0, s_acc, layer, *, chunk, t_valid, nb):
    bsz, tp, width = qkv.shape
    assert bsz % nb == 0 and tp % chunk == 0
    blk = (nb, GDN_HEADS, GDN_HEAD_DIM, GDN_HEAD_DIM)
    state = pl.BlockSpec((None,) + blk, lambda b, c: (layer, b, 0, 0, 0))
    first = s_acc is None
    out_state = pl.BlockSpec((s0.shape[0],) + blk, lambda b, c: (0, b, 0, 0, 0)) if first else state
    nc = tp // chunk
    cb = chunk // SUBLANES
    dv = GDN_HEADS * GDN_HEAD_DIM
    (conv_w, conv_w_spec), (alr, alr_spec), (dbr, dbr_spec) = (_whole(x, 2) for x in (conv_w, alr, dbr))
    return pl.pallas_call(
        functools.partial(_gdn_rec_body, chunk=chunk, t_valid=t_valid, nb=nb, layer=layer, first_writer=first),
        grid=(bsz // nb, nc),
        in_specs=[pl.BlockSpec((nb, chunk, GDN_QKV), lambda b, c: (b, c, 0)),
                  pl.BlockSpec((nb, SUBLANES, GDN_QKV), lambda b, c: (b, jnp.maximum(c * cb - 1, 0), 0)),
                  pl.BlockSpec((nb, SUBLANES, GDN_QKV), lambda b, c: (b, 0, 0)),
                  pl.BlockSpec((nb, chunk, LANES), lambda b, c: (b, c, 0)),
                  conv_w_spec, alr_spec, dbr_spec,
                  state] + ([] if first else [pl.BlockSpec(memory_space=pl.ANY)]),
        out_specs=[pl.BlockSpec((nb, chunk, dv), lambda b, c: (b, c, 0)), out_state],
        out_shape=[jax.ShapeDtypeStruct((bsz, tp, dv), F32), jax.ShapeDtypeStruct(s0.shape, F32)],
        input_output_aliases={} if first else {8: 1},
        scratch_shapes=[pltpu.VMEM((nb, chunk + SUBLANES, GDN_QKV), F32),
                        pltpu.VMEM((nb * dv // MXU_TILE, MXU_TILE, MXU_TILE), F32)]
        + _block_diag_scratch(chunk, nb, GDN_HEADS, GDN_HEAD_DIM),
        compiler_params=_params("parallel", "arbitrary"),
        name="gdn_rec",
    )(qkv, qkv, conv_init, ba, conv_w, alr, dbr, s0, *([] if first else [s_acc]))


def _rwkv_proj_body(*refs, has_vres, seq_len, tm):
    if has_vres:
        (x_ref, prev_ref, start_ref, vf_ref, lerp_ref, wr_ref, wk_ref, wv_ref, w0_ref, w1_ref, w2_ref, a0_ref,
         a1_ref, a2_ref, g1_ref, g2_ref, v0_ref, v1_ref, v2_ref, r_ref, k_ref, v_ref, lw_ref, a_ref, g_ref) = refs
    else:
        (x_ref, prev_ref, start_ref, lerp_ref, wr_ref, wk_ref, wv_ref, w0_ref, w1_ref, w2_ref, a0_ref, a1_ref,
         a2_ref, g1_ref, g2_ref, r_ref, k_ref, v_ref, lw_ref, a_ref, g_ref) = refs
    x = x_ref[...]
    rolled = pltpu.roll(x, 1, axis=0)
    rows = lax.broadcasted_iota(jnp.int32, (tm, 1), 0)
    if tm <= seq_len:
        i = pl.program_id(0)
        tiles_per_seq = seq_len // tm
        carried = start_ref[pl.ds(i // tiles_per_seq, 1), :]
        before = prev_ref[SUBLANES - 1:SUBLANES, :]
        first = jnp.where(i % tiles_per_seq == 0, carried, before)
        xp = jnp.where(rows == 0, first, rolled)
    else:
        xp = jnp.where((rows & (seq_len - 1)) == 0, start_ref[...], rolled)
    dx = xp - x

    def mix(s):
        return (x + dx * lerp_ref[s:s + 1, :]).astype(BF16)

    r_ref[...] = jnp.dot(mix(0), wr_ref[...], preferred_element_type=F32)
    k_ref[...] = jnp.dot(mix(1), wk_ref[...], preferred_element_type=F32)
    xv = mix(2)
    v = jnp.dot(xv, wv_ref[...], preferred_element_type=F32)
    if has_vres:
        gate = jax.nn.sigmoid(v0_ref[...] + _dot(jnp.dot(xv, v1_ref[...], preferred_element_type=F32), v2_ref[...]))
        v = v + (vf_ref[...] - v) * gate
    v_ref[...] = v
    dec = w0_ref[...] + _dot(jnp.tanh(jnp.dot(mix(3), w1_ref[...], preferred_element_type=F32)), w2_ref[...])
    lw_ref[...] = -math.exp(-0.5) * jax.nn.sigmoid(dec)
    a_ref[...] = jax.nn.sigmoid(a0_ref[...] + _dot(jnp.dot(mix(4), a1_ref[...], preferred_element_type=F32),
                                                   a2_ref[...]))
    g_ref[...] = _dot(jax.nn.sigmoid(jnp.dot(mix(5), g1_ref[...], preferred_element_type=F32)), g2_ref[...])


def _rwkv_proj(x, shift, v_first, lerp, wr, wk, wv, w0, w1, w2, a0, a1, a2, g1, g2, vres, *, seq_len, tm):
    n, d = x.shape
    has_vres = vres is not None
    row = pl.BlockSpec((tm, d), lambda i: (i, 0))
    full = lambda a: _whole(a, 1, pipeline_mode=pl.Buffered(1))
    tb = tm // SUBLANES
    if tm <= seq_len:
        assert seq_len % tm == 0
        start = full(shift)
    else:
        assert tm % seq_len == 0 and seq_len & (seq_len - 1) == 0
        start = (jnp.repeat(shift.arr[shift.idx], seq_len, axis=0), row)
    prev_spec = pl.BlockSpec((SUBLANES, d), lambda i: (jnp.maximum(i * tb - 1, 0), 0))
    ops = ([(x, row), (x, prev_spec), start] + ([(v_first, row)] if has_vres else [])
           + [full(a) for a in (lerp, wr, wk, wv, w0, w1, w2, a0, a1, a2, g1, g2) + (tuple(vres) if has_vres else ())])
    args = [op for op, _ in ops]
    in_specs = [spec for _, spec in ops]
    return pl.pallas_call(
        functools.partial(_rwkv_proj_body, has_vres=has_vres, seq_len=seq_len, tm=tm),
        grid=(n // tm,),
        in_specs=in_specs,
        out_specs=[row] * 6,
        out_shape=[jax.ShapeDtypeStruct((n, d), F32)] * 6,
        compiler_params=_params("parallel"),
        name="rwkv_proj_vres" if has_vres else "rwkv_proj",
    )(*args)


def _rwkv_rec_body(r_ref, k_ref, v_ref, lw_ref, a_ref, kk_ref, ka_ref, rk_ref, lnw_ref, lnb_ref, s0_ref,
                   *rest, chunk, t_valid, heads, nb, layer, first_writer):
    y_ref, sout_ref, s_sc, *bd_scratch = rest if first_writer else rest[1:]
    sout = sout_ref.at[layer] if first_writer else sout_ref
    c = pl.program_id(1)
    hd = RWKV_HEAD
    d = heads * hd
    gw = MXU_TILE
    sgh = gw // hd
    nsg = heads // sgh
    hg = min(heads, gw // chunk)
    ntg = heads // hg
    kt = hg * hd
    sg_per_tg = kt // gw

    @pl.when(c == 0)
    def _():
        s_sc[...] = jnp.zeros_like(s_sc)
        for q in range(nb):
            for h in range(heads):
                g, j = divmod(h, sgh)
                s_sc[q * nsg + g, j * hd:(j + 1) * hd, j * hd:(j + 1) * hd] = s0_ref[q, h]
        for ref in bd_scratch:
            ref[...] = jnp.zeros_like(ref)

    state_mask = _block_mask(gw, gw, hd, hd)
    ones_bd = state_mask.astype(BF16)

    def head_sums(xs):
        parts = []
        for x in xs:
            for g in range(nsg):
                parts.extend(_split2(x[:, g * gw:(g + 1) * gw]))
        sums = _shared_rhs_dots(parts, ones_bd)
        return [jnp.concatenate([sums[2 * (i * nsg + g)] + sums[2 * (i * nsg + g) + 1] for g in range(nsg)], axis=1)
                for i in range(len(xs))]

    incl, _ = _tri_incl(chunk)
    tri_bf = incl.astype(BF16)

    def load(q):
        r_all = r_ref[q]
        k_all = k_ref[q]
        v_all = v_ref[q]
        lw_all = lw_ref[q]
        a_all = a_ref[q]
        if t_valid < chunk:
            row_ok = lax.broadcasted_iota(jnp.int32, (chunk, 1), 0) < t_valid
            r_all = jnp.where(row_ok, r_all, 0.0)
            k_all = jnp.where(row_ok, k_all, 0.0)
            v_all = jnp.where(row_ok, v_all, 0.0)
            lw_all = jnp.where(row_ok, lw_all, 0.0)
        kx = k_all * kk_ref[...]
        kmod = k_all * (1.0 + (a_all - 1.0) * ka_ref[...])
        return dict(r=r_all, v=v_all, lw=lw_all, a=a_all, kx=kx, kmod=kmod)

    seqs = [load(q) for q in range(nb)]
    sums = head_sums([s["kx"] * s["kx"] for s in seqs] + [s["r"] * s["kmod"] * rk_ref[...] for s in seqs])

    def prep(q):
        s = seqs[q]
        gc = _cumsum_rows(tri_bf, s["lw"])
        kk = s["kx"] * lax.rsqrt(sums[q] + L2_EPS)
        bv = kk * s["a"]
        e_neg = jnp.exp(-gc)
        g_last = gc[chunk - 1:chunk, :]
        e_rem = jnp.exp(g_last - gc)
        return dict(v=s["v"], rk_sum=sums[nb + q], at=-kk * jnp.exp(gc - s["lw"]), rt=s["r"] * jnp.exp(gc),
                    bt=bv * e_neg, kt=s["kmod"] * e_neg, bp=bv * e_rem, kp=s["kmod"] * e_rem, dec=jnp.exp(g_last))

    seqs = [prep(q) for q in range(nb)]

    bdc_ref, bdt_ref = bd_scratch if bd_scratch else (None, None)
    bd_c = _BlockDiag(chunk, hg, hd, nb * ntg, bdc_ref)
    bd_t = _BlockDiag(chunk, hg, chunk, nb * ntg, bdt_ref)
    ri = lax.broadcasted_iota(jnp.int32, (chunk, hg * chunk), 0)
    cj = lax.broadcasted_iota(jnp.int32, (chunk, hg * chunk), 1) & (chunk - 1)
    strict_w = ri > cj
    incl_w = ri >= cj

    tgs = range(nb * ntg)
    lane = lambda name, t: seqs[t // ntg][name][:, (t % ntg) * kt:(t % ntg + 1) * kt]
    lhs = [jnp.concatenate([lane("at", t), lane("rt", t)], axis=0).astype(BF16) for t in tgs]
    mb = [lax.dot_general(lhs[t], bd_c(lane("bt", t), 0, t), (((1,), (1,)), ((), ())),
                          preferred_element_type=F32) for t in tgs]
    mk = [lax.dot_general(lhs[t], bd_c(lane("kt", t), 1, t), (((1,), (1,)), ((), ())),
                          preferred_element_type=F32) for t in tgs]
    a_ab = [jnp.where(strict_w, mb[t][:chunk], 0.0) for t in tgs]
    a_rb = [jnp.where(incl_w, mb[t][chunk:], 0.0) for t in tgs]
    a_ak = [jnp.where(strict_w, mk[t][:chunk], 0.0) for t in tgs]
    a_rk = [jnp.where(incl_w, mk[t][chunk:], 0.0) for t in tgs]
    t_mat = _unit_lower_inverse_wide([-x for x in a_ab], chunk, hg, bd_t)

    xs = []
    for t in tgs:
        parts = []
        for q in range(sg_per_tg):
            g = t * sg_per_tg + q
            parts.append(lax.dot_general(lhs[t][:, q * gw:(q + 1) * gw], s_sc[g].astype(BF16),
                                         (((1,), (1,)), ((), ())), preferred_element_type=F32))
        xs.append(parts[0] if len(parts) == 1 else jnp.concatenate(parts, axis=1))
    av = [_shared_rhs_dots([a_ak[t], a_rk[t]], bd_c(lane("v", t), 2, t)) for t in tgs]
    rhs_u = [xs[t][:chunk] + av[t][0] for t in tgs]
    u = [jnp.dot(t_mat[t].astype(BF16), bd_c(rhs_u[t], 0, t), preferred_element_type=F32) for t in tgs]
    y = [xs[t][chunk:] + jnp.dot(a_rb[t].astype(BF16), bd_c(u[t], 1, t), preferred_element_type=F32) + av[t][1]
         for t in tgs]

    join = lambda xs_, q: xs_[q] if ntg == 1 else jnp.concatenate(xs_[q * ntg:(q + 1) * ntg], axis=1)
    for q in range(nb):
        sq = seqs[q]
        u_all = join(u, q)
        for g in range(nsg):
            sl = slice(g * gw, (g + 1) * gw)
            upd = _dot_tn(jnp.concatenate([u_all[:, sl], sq["v"][:, sl]], axis=0),
                          jnp.concatenate([sq["bp"][:, sl], sq["kp"][:, sl]], axis=0))
            s_sc[q * nsg + g] = s_sc[q * nsg + g] * sq["dec"][:, sl] + jnp.where(state_mask, upd, 0.0)

    y_all = [join(y, q) for q in range(nb)]
    mean = head_sums(y_all)
    yc = [y_all[q] - mean[q] * (1.0 / hd) for q in range(nb)]
    var = head_sums([x * x for x in yc])
    for q in range(nb):
        yn = yc[q] * lax.rsqrt(var[q] * (1.0 / hd) + GN_EPS)
        y_ref[q] = yn * lnw_ref[...] + lnb_ref[...] + seqs[q]["rk_sum"] * seqs[q]["v"]

    @pl.when(c == pl.num_programs(1) - 1)
    def _():
        if first_writer:
            sout_ref[...] = jnp.zeros_like(sout_ref)
        for q in range(nb):
            for h in range(heads):
                g, j = divmod(h, sgh)
                sout[q, h] = s_sc[q * nsg + g, j * hd:(j + 1) * hd, j * hd:(j + 1) * hd]


def _rwkv_rec(r, k, v, lw, a, k_k, k_a, r_k, ln_w, ln_b, s0, s_acc, layer, *, chunk, t_valid, nb):
    bsz, tp, d = r.shape
    assert bsz % nb == 0 and tp % chunk == 0
    heads = d // RWKV_HEAD
    nc = tp // chunk
    seq = pl.BlockSpec((nb, chunk, d), lambda b, c: (b, c, 0))
    vecs = [_whole(x, 2) for x in (k_k, k_a, r_k, ln_w, ln_b)]
    blk = (nb, heads, RWKV_HEAD, RWKV_HEAD)
    st = pl.BlockSpec((None,) + blk, lambda b, c: (layer, b, 0, 0, 0))
    first = s_acc is None
    out_st = pl.BlockSpec((s0.shape[0],) + blk, lambda b, c: (0, b, 0, 0, 0)) if first else st
    return pl.pallas_call(
        functools.partial(_rwkv_rec_body, chunk=chunk, t_valid=t_valid, heads=heads, nb=nb, layer=layer,
                          first_writer=first),
        grid=(bsz // nb, nc),
        in_specs=[seq] * 5 + [spec for _, spec in vecs] + [st] + ([] if first else [pl.BlockSpec(memory_space=pl.ANY)]),
        out_specs=[seq, out_st],
        out_shape=[jax.ShapeDtypeStruct((bsz, tp, d), F32), jax.ShapeDtypeStruct(s0.shape, F32)],
        input_output_aliases={} if first else {11: 1},
        scratch_shapes=[pltpu.VMEM((nb * d // MXU_TILE, MXU_TILE, MXU_TILE), F32)]
        + _block_diag_scratch(chunk, nb, heads, RWKV_HEAD),
        compiler_params=_params("parallel", "arbitrary"),
        name="rwkv_rec",
    )(r, k, v, lw, a, *[op for op, _ in vecs], s0, *([] if first else [s_acc]))


def _pad_time(x, tp):
    t = x.shape[1]
    if t == tp:
        return x
    return jnp.pad(x, [(0, 0), (0, tp - t)] + [(0, 0)] * (x.ndim - 2))


def _tiles(n):
    return 1024 if n % 1024 == 0 else n


def _trunk(h, bsz, t, gdn_s, gdn_buf, rw_s, rw_shift, p, chunk, nb):
    n, d = h.shape
    depth = p["norm_g"].shape[0]
    tm = _tiles(n)
    tp = -(-t // chunk) * chunk
    out_gdn_buf, out_rw_shift = [], []
    out_gdn_s = out_rw_s = None
    v_first = None
    for i in range(depth):
        row = lambda s: _Layer(p["norm_g"], i, s)
        h, u = _ffn(h, row(0), row(1), row(2), p["ffn_w_in"], p["ffn_w_out"], i, 0,
                    emit_u=True, tm=min(tm, 512), tf=256, u_dtype=BF16 if i % 2 == 0 else F32)
        j = i // 2
        if i % 2 == 0:
            at = lambda name: _Layer(p[name], j)
            main, ba = _gdn_proj(u, at("gdn_w_main"), at("gdn_w_ba"), tm=min(tm, 512))
            main3 = main.reshape(bsz, t, main.shape[1])
            out_gdn_buf.append(main3[:, t - (GDN_CONV - 1):, :GDN_QKV])
            ba3 = _pad_time(ba.reshape(bsz, t, LANES), tp)
            conv_init = jnp.pad(gdn_buf[j], [(0, 0), (SUBLANES - (GDN_CONV - 1), 0), (0, 0)])
            qkv = main3 if tp == t else _pad_time(main3[:, :, :GDN_QKV], tp)
            o, out_gdn_s = _gdn_rec(qkv, conv_init, ba3, at("gdn_conv_w"), at("gdn_alr"), at("gdn_dbr"),
                                    gdn_s, out_gdn_s, j, chunk=chunk, t_valid=min(t, chunk), nb=nb)
            o2 = o[:, :t].reshape(n, o.shape[2])
            mixer, mixer_args = "gdn", (o2, main, at("gdn_o_norm"), at("gdn_w_out"), row(3))
        else:
            u3 = u.reshape(bsz, t, d)
            out_rw_shift.append(u3[:, -1])
            at = lambda name: _Layer(p["rwkv_" + name], j)
            vres = None if j == 0 else tuple(_Layer(p["rwkv_" + name], j - 1) for name in ("v0", "v1", "v2"))
            w_rkv = [_Layer(p["rwkv_w_rkv"], j, s) for s in range(3)]
            r, k, v, lw, a, g = _rwkv_proj(
                u, _Layer(rw_shift, j), v_first, at("lerp"), *w_rkv, at("w0"), at("w1"), at("w2"), at("a0"),
                at("a1"), at("a2"), at("g1"), at("g2"), vres, seq_len=t, tm=min(tm, 512))
            if v_first is None:
                v_first = v
            seq = lambda x: _pad_time(x.reshape(bsz, t, d), tp)
            y, out_rw_s = _rwkv_rec(seq(r), seq(k), seq(v), seq(lw), seq(a), at("k_k"), at("k_a"),
                                    at("r_k"), at("ln_w"), at("ln_b"), rw_s, out_rw_s, j,
                                    chunk=chunk, t_valid=min(t, chunk), nb=nb)
            mixer, mixer_args = "rwkv", (y[:, :t].reshape(n, d), g, at("w_o"), row(3))
        h = _ffn(h, row(4), row(5), row(5), p["ffn_w_in"], p["ffn_w_out"], i, 1,
                 emit_u=False, tm=min(tm, 512), tf=256, mixer=mixer, mixer_args=mixer_args)[0]
    return h.reshape(bsz, t, d), out_gdn_s, jnp.stack(out_gdn_buf), out_rw_s, jnp.stack(out_rw_shift)


def kernel(x_prompt, x_sample, state_gdn, state_gdn_conv, state_rwkv, state_rwkv_shift, norm_g, ffn_w_in, ffn_w_out, gdn_w_in, gdn_conv_w, gdn_a_log, gdn_dt_bias, gdn_o_norm, gdn_w_out, rwkv_lerp, rwkv_w_rkv, rwkv_w0, rwkv_w1, rwkv_w2, rwkv_a0, rwkv_a1, rwkv_a2, rwkv_v0, rwkv_v1, rwkv_v2, rwkv_g1, rwkv_g2, rwkv_k_k, rwkv_k_a, rwkv_r_k, rwkv_ln_w, rwkv_ln_b, rwkv_w_o):
    bsz, t, d = x_prompt.shape
    dbsz, dt, _ = x_sample.shape
    n_gdn = gdn_w_in.shape[0]
    n_rwkv = rwkv_w_rkv.shape[0]
    nh = GDN_HEADS
    n_main = GDN_QKV + nh * GDN_HEAD_DIM

    def lane_slot(x, offset):
        return jnp.pad(x, [(0, 0), (offset, LANES - offset - nh)])[:, None, :]

    row3 = lambda x: x[:, None, :]
    p = dict(
        norm_g=norm_g[:, :, None, :],
        ffn_w_in=ffn_w_in.astype(BF16),
        ffn_w_out=ffn_w_out.astype(BF16),
        gdn_w_main=gdn_w_in[:, :, :n_main].astype(BF16),
        gdn_w_ba=jnp.pad(gdn_w_in[:, :, n_main:], [(0, 0), (0, 0), (0, LANES - 2 * nh)]).astype(BF16),
        gdn_conv_w=gdn_conv_w,
        gdn_alr=lane_slot(gdn_a_log, nh), gdn_dbr=lane_slot(gdn_dt_bias, nh),
        gdn_o_norm=row3(gdn_o_norm),
        gdn_w_out=gdn_w_out.astype(BF16),
        rwkv_lerp=jnp.pad(rwkv_lerp, [(0, 0), (0, SUBLANES - rwkv_lerp.shape[1]), (0, 0)]),
        rwkv_w_rkv=rwkv_w_rkv.astype(BF16),
        rwkv_w0=row3(rwkv_w0), rwkv_w1=rwkv_w1.astype(BF16), rwkv_w2=rwkv_w2.astype(BF16),
        rwkv_a0=row3(rwkv_a0), rwkv_a1=rwkv_a1.astype(BF16), rwkv_a2=rwkv_a2.astype(BF16),
        rwkv_v0=row3(rwkv_v0), rwkv_v1=rwkv_v1.astype(BF16), rwkv_v2=rwkv_v2.astype(BF16),
        rwkv_g1=rwkv_g1.astype(BF16), rwkv_g2=rwkv_g2.astype(BF16),
        rwkv_k_k=row3(rwkv_k_k), rwkv_k_a=row3(rwkv_k_a),
        rwkv_r_k=rwkv_r_k.reshape(n_rwkv, 1, d),
        rwkv_ln_w=row3(rwkv_ln_w), rwkv_ln_b=row3(rwkv_ln_b),
        rwkv_w_o=rwkv_w_o.astype(BF16),
    )
    dtype = x_prompt.dtype
    zero_gdn = jnp.zeros((n_gdn, bsz) + state_gdn.shape[2:], dtype)
    zero_conv = jnp.zeros((n_gdn, bsz) + state_gdn_conv.shape[2:], dtype)
    zero_rwkv = jnp.zeros((n_rwkv, bsz) + state_rwkv.shape[2:], dtype)
    zero_shift = jnp.zeros((n_rwkv, bsz, d), dtype)
    prompt = _trunk(x_prompt.reshape(bsz * t, d), bsz, t, zero_gdn, zero_conv, zero_rwkv, zero_shift, p,
                    chunk=64, nb=4)
    sample = _trunk(x_sample.reshape(dbsz * dt, d), dbsz, dt, state_gdn, state_gdn_conv, state_rwkv,
                    state_rwkv_shift, p, chunk=SUBLANES, nb=8)
    return (prompt[0], sample[0]) + prompt[1:] + sample[1:]
```

```python
import functools
import math

import jax
import jax.numpy as jnp
from jax import lax
from jax.experimental import pallas as pl
from jax.experimental.pallas import tpu as pltpu

F32 = jnp.float32
BF16 = jnp.bfloat16

NORM_EPS = 1e-6
L2_EPS = 1e-6
GN_EPS = 64e-5

GDN_HEADS = 8
GDN_HEAD_DIM = 128
GDN_QKV = 3 * GDN_HEADS * GDN_HEAD_DIM
GDN_CONV = 4
RWKV_HEAD = 64

V7X_VMEM_BYTES = 64 * 1024 * 1024
VMEM_LIMIT_BYTES = V7X_VMEM_BYTES * 7 // 8
SUBLANES = 8
LANES = 128
BF16_ROWS = 16
MXU_TILE = 256


def _params(*semantics):
    return pltpu.CompilerParams(dimension_semantics=semantics, vmem_limit_bytes=VMEM_LIMIT_BYTES)


class _Layer:
    def __init__(self, arr, *idx):
        self.arr, self.idx = arr, idx

    @property
    def shape(self):
        return self.arr.shape[len(self.idx):]


def _whole(x, grid_rank, **spec_kwargs):
    arr, idx = (x.arr, x.idx) if isinstance(x, _Layer) else (x, ())
    index = idx + (0,) * (arr.ndim - len(idx))
    index_map = (lambda i: index) if grid_rank == 1 else (lambda i, j: index)
    return arr, pl.BlockSpec((None,) * len(idx) + arr.shape[len(idx):], index_map, **spec_kwargs)


def _rms(x, g):
    return x * lax.rsqrt(jnp.mean(x * x, axis=-1, keepdims=True) + NORM_EPS) * g


def _softplus(x):
    return jnp.maximum(x, 0.0) + jnp.log1p(jnp.exp(-jnp.abs(x)))


def _dot(a, b):
    return jnp.dot(a.astype(BF16), b.astype(BF16), preferred_element_type=F32)


def _dot_tn(a, b):
    return lax.dot_general(a.astype(BF16), b.astype(BF16), (((0,), (0,)), ((), ())),
                           preferred_element_type=F32)


def _split2(a):
    hi = a.astype(BF16)
    lo = (a - hi.astype(F32)).astype(BF16)
    return hi, lo


def _split3(a):
    hi = a.astype(BF16)
    r = a - hi.astype(F32)
    mid = r.astype(BF16)
    lo = (r - mid.astype(F32)).astype(BF16)
    return hi, mid, lo


def _tri_incl(c):
    ri = lax.broadcasted_iota(jnp.int32, (c, c), 0)
    ci = lax.broadcasted_iota(jnp.int32, (c, c), 1)
    return ri >= ci, ri > ci


def _cumsum_rows(tri_bf, x):
    hi, mid, lo = _split3(x)
    return (jnp.dot(tri_bf, hi, preferred_element_type=F32)
            + (jnp.dot(tri_bf, mid, preferred_element_type=F32) + jnp.dot(tri_bf, lo, preferred_element_type=F32)))


def _log2(n):
    assert n & (n - 1) == 0, n
    return n.bit_length() - 1


def _block_mask(rows, cols, rblk, cblk):
    ri = lax.broadcasted_iota(jnp.int32, (rows, cols), 0)
    ci = lax.broadcasted_iota(jnp.int32, (rows, cols), 1)
    return (ri >> _log2(rblk)) == (ci >> _log2(cblk))


def _tile_rows(x, n):
    return x if n == 1 else jnp.concatenate([x] * n, axis=0)


class _BlockDiag:
    def __init__(self, c, n, cb, units, ref=None):
        self.c, self.n, self.cb, self.units, self.ref = c, n, cb, units, ref
        self.mask = None if ref is not None else _block_mask(n * c, n * cb, c, cb)

    def __call__(self, x, site, unit):
        c, n, cb = self.c, self.n, self.cb
        if self.ref is None:
            return jnp.where(self.mask, _tile_rows(x.astype(F32), n), 0.0).astype(BF16)
        slot = site * self.units + unit
        xb = x.astype(BF16)
        for h in range(n):
            self.ref[slot, h * c:(h + 1) * c, h * cb:(h + 1) * cb] = xb[:, h * cb:(h + 1) * cb]
        return self.ref[slot]


def _shared_rhs_dots(xs, rhs):
    m = xs[0].shape[0]
    if len(xs) == 1 or m % BF16_ROWS != 0:
        return [jnp.dot(x.astype(BF16), rhs, preferred_element_type=F32) for x in xs]
    t = jnp.dot(jnp.concatenate([x.astype(BF16) for x in xs], axis=0), rhs, preferred_element_type=F32)
    return [t[i * m:(i + 1) * m] for i in range(len(xs))]


INVERSE_SITES = 2
CHANNEL_SITES = 3


def _unit_lower_inverse_wide(mats, c, n, bd):
    ri = lax.broadcasted_iota(jnp.int32, (c, n * c), 0)
    cj = lax.broadcasted_iota(jnp.int32, (c, n * c), 1) & (c - 1)
    eye = (ri == cj).astype(F32)
    base = (ri >> 3) == (cj >> 3)
    idx = range(len(mats))
    nm = [jnp.where(base, -a, 0.0) for a in mats]
    p = [eye + x for x in nm]
    n2 = [_shared_rhs_dots([nm[i]], bd(nm[i], 1, i))[0] for i in idx]
    pn = [_shared_rhs_dots([p[i], n2[i]], bd(n2[i], 1, i)) for i in idx]
    p = [p[i] + pn[i][0] for i in idx]
    p = [p[i] + _shared_rhs_dots([p[i]], bd(pn[i][1], 1, i))[0] for i in idx]
    if c > SUBLANES:
        a_bd = [bd(mats[i], 0, i) for i in idx]
    s = SUBLANES
    while s < c:
        pa = [_shared_rhs_dots([p[i]], a_bd[i])[0] for i in idx]
        e = [_shared_rhs_dots([pa[i]], bd(p[i], 1, i))[0] for i in idx]
        shift = _log2(2 * s)
        merge = ((ri >> shift) == (cj >> shift)) & ((ri & s) != 0) & ((cj & s) == 0)
        p = [p[i] - jnp.where(merge, e[i], 0.0) for i in idx]
        s *= 2
    return p


def _block_diag_scratch(chunk, nb, heads, head_dim):
    if chunk % BF16_ROWS != 0:
        return []
    hg = min(heads, MXU_TILE // chunk)
    units = nb * (heads // hg)
    return [pltpu.VMEM((CHANNEL_SITES * units, hg * chunk, hg * head_dim), BF16),
            pltpu.VMEM((INVERSE_SITES * units, hg * chunk, hg * chunk), BF16)]


def _gdn_mixer_out(o_ref, z_ref, on_ref, w_ref):
    hd = GDN_HEAD_DIM
    parts = []
    for h in range(GDN_HEADS):
        sl = slice(h * hd, (h + 1) * hd)
        o = o_ref[:, sl]
        z = z_ref[:, sl]
        on = o * lax.rsqrt(jnp.mean(o * o, axis=-1, keepdims=True) + NORM_EPS) * on_ref[...]
        parts.append((on * (z * jax.nn.sigmoid(z))).astype(BF16))
    return jnp.dot(jnp.concatenate(parts, axis=1), w_ref[...], preferred_element_type=F32)


def _rwkv_mixer_out(y_ref, gate_ref, w_ref):
    return jnp.dot((y_ref[...] * gate_ref[...]).astype(BF16), w_ref[...], preferred_element_type=F32)


_MIXER_REFS = {None: 0, "gdn": 5, "rwkv": 4}


def _ffn_body(*refs, mixer, emit_u, tf):
    n_mix = _MIXER_REFS[mixer]
    mix, (h_ref, gin_ref, gout_ref, gnext_ref, wi_ref, wo_ref, hout_ref, *rest) = refs[:n_mix], refs[n_mix:]
    f = wo_ref.shape[0]
    x = h_ref[...]
    if mixer == "gdn":
        x = x + _rms(_gdn_mixer_out(*mix[:4]), mix[4][...])
    elif mixer == "rwkv":
        x = x + _rms(_rwkv_mixer_out(*mix[:3]), mix[3][...])
    xn = _rms(x, gin_ref[...]).astype(BF16)
    acc = None
    for c0 in range(0, f, tf):
        gate = jnp.dot(xn, wi_ref[:, c0:c0 + tf], preferred_element_type=F32)
        up = jnp.dot(xn, wi_ref[:, f + c0:f + c0 + tf], preferred_element_type=F32)
        act = ((gate * jax.nn.sigmoid(gate)) * up).astype(BF16)
        part = jnp.dot(act, wo_ref[c0:c0 + tf, :], preferred_element_type=F32)
        acc = part if acc is None else acc + part
    hn = x + 0.5 * _rms(acc, gout_ref[...])
    hout_ref[...] = hn
    if emit_u:
        rest[0][...] = _rms(hn, gnext_ref[...]).astype(rest[0].dtype)


def _ffn(h, g_in, g_out, g_next, w_in, w_out, layer, half, *, emit_u, tm, tf, u_dtype=F32, mixer=None,
         mixer_args=()):
    n, d = h.shape
    f = w_out.shape[2]
    row = pl.BlockSpec((tm, d), lambda i: (i, 0))
    resident = dict(pipeline_mode=pl.Buffered(1))
    if mixer == "gdn":
        o, main, o_norm, w_mix, g_mix = mixer_args
        dv = o.shape[1]
        mixer_ops = [(o, pl.BlockSpec((tm, dv), lambda i: (i, 0))),
                     (main, pl.BlockSpec((tm, dv), lambda i: (i, GDN_QKV // dv))),
                     _whole(o_norm, 1), _whole(w_mix, 1, **resident), _whole(g_mix, 1)]
    elif mixer == "rwkv":
        y, gate, w_mix, g_mix = mixer_args
        mixer_ops = [(y, row), (gate, row), _whole(w_mix, 1, **resident), _whole(g_mix, 1)]
    else:
        mixer_ops = []
    assert len(mixer_ops) == _MIXER_REFS[mixer]
    ops = mixer_ops + [(h, row), _whole(g_in, 1), _whole(g_out, 1), _whole(g_next, 1),
                       (w_in, pl.BlockSpec((None, None, d, 2 * f), lambda i: (layer, half, 0, 0), **resident)),
                       (w_out, pl.BlockSpec((None, None, f, d), lambda i: (layer, half, 0, 0), **resident))]
    out_shape = [jax.ShapeDtypeStruct((n, d), F32)]
    out_specs = [row]
    if emit_u:
        out_shape.append(jax.ShapeDtypeStruct((n, d), u_dtype))
        out_specs.append(row)
    return pl.pallas_call(
        functools.partial(_ffn_body, mixer=mixer, emit_u=emit_u, tf=tf),
        grid=(n // tm,),
        in_specs=[spec for _, spec in ops],
        out_specs=out_specs,
        out_shape=out_shape,
        compiler_params=_params("parallel"),
        name="ffn_u" if emit_u else "ffn_" + mixer,
    )(*[op for op, _ in ops])


def _gdn_proj_body(u_ref, w_ref, wba_ref, main_ref, ba_ref):
    xb = u_ref[...].astype(BF16)
    main_ref[...] = jnp.dot(xb, w_ref[...], preferred_element_type=F32)
    ba_ref[...] = jnp.dot(xb, wba_ref[...], preferred_element_type=F32)


def _gdn_proj(u, w_main, w_ba, *, tm):
    n, d = u.shape
    nm = w_main.shape[1]
    resident = dict(pipeline_mode=pl.Buffered(1))
    w_main, w_main_spec = _whole(w_main, 1, **resident)
    w_ba, w_ba_spec = _whole(w_ba, 1, **resident)
    return pl.pallas_call(
        _gdn_proj_body,
        grid=(n // tm,),
        in_specs=[pl.BlockSpec((tm, d), lambda i: (i, 0)), w_main_spec, w_ba_spec],
        out_specs=[pl.BlockSpec((tm, nm), lambda i: (i, 0)),
                   pl.BlockSpec((tm, LANES), lambda i: (i, 0))],
        out_shape=[jax.ShapeDtypeStruct((n, nm), F32), jax.ShapeDtypeStruct((n, LANES), F32)],
        compiler_params=_params("parallel"),
        name="gdn_proj",
    )(u, w_main, w_ba)


def _gdn_rec_body(cur_ref, prev_ref, init_ref, ba_ref, cw_ref, alr_ref, dbr_ref,
                  s0_ref, *rest, chunk, t_valid, nb, layer, first_writer):
    o_ref, sout_ref, ext_sc, s_sc, *bd_scratch = rest if first_writer else rest[1:]
    sout = sout_ref.at[layer] if first_writer else sout_ref
    c = pl.program_id(1)
    hd = GDN_HEAD_DIM
    nh = GDN_HEADS
    sgh = MXU_TILE // hd
    nsg = nh // sgh

    @pl.when(c == 0)
    def _():
        s_sc[...] = jnp.zeros_like(s_sc)
        for q in range(nb):
            for h in range(nh):
                g, j = divmod(h, sgh)
                s_sc[q * nsg + g, j * hd:(j + 1) * hd, j * hd:(j + 1) * hd] = s0_ref[q, h]
        for ref in bd_scratch:
            ref[...] = jnp.zeros_like(ref)
        ext_sc[:, 0:SUBLANES, :] = init_ref[...]

    @pl.when(c > 0)
    def _():
        ext_sc[:, 0:SUBLANES, :] = prev_ref[...]

    ext_sc[:, SUBLANES:SUBLANES + chunk, :] = cur_ref[...]
    incl, _ = _tri_incl(chunk)
    tri_bf = incl.astype(BF16)

    kb, qn, kn, vb, kbeg, qdec, kdec, gcol, dlast = ([] for _ in range(9))
    for s_ in range(nb):
        ext = ext_sc[s_]
        conv = cw_ref[GDN_CONV - 1:GDN_CONV, :] * ext[SUBLANES:]
        for back in range(1, GDN_CONV):
            earlier = pltpu.roll(ext, back, axis=0)[SUBLANES:]
            conv = conv + cw_ref[GDN_CONV - 1 - back:GDN_CONV - back, :] * earlier
        qkv = conv * jax.nn.sigmoid(conv)

        ba = ba_ref[s_]
        g_cols = -jnp.exp(alr_ref[...]) * _softplus(ba + dbr_ref[...])
        beta_cols = jax.nn.sigmoid(ba)
        if t_valid < chunk:
            row_ok = lax.broadcasted_iota(jnp.int32, (chunk, 1), 0) < t_valid
            qkv = jnp.where(row_ok, qkv, 0.0)
            g_cols = jnp.where(row_ok, g_cols, 0.0)
        gc_cols = _cumsum_rows(tri_bf, g_cols)

        for h in range(nh):
            q = qkv[:, h * hd:(h + 1) * hd]
            k = qkv[:, (nh + h) * hd:(nh + h + 1) * hd]
            v = qkv[:, (2 * nh + h) * hd:(2 * nh + h + 1) * hd]
            qn_h = q * lax.rsqrt(jnp.sum(q * q, axis=-1, keepdims=True) + L2_EPS) * (hd ** -0.5)
            kn_h = k * lax.rsqrt(jnp.sum(k * k, axis=-1, keepdims=True) + L2_EPS)
            beta = beta_cols[:, h:h + 1]
            gc = gc_cols[:, nh + h:nh + h + 1]
            eg = jnp.exp(gc)
            g_last = gc[chunk - 1:chunk, :]
            kb_h = kn_h * beta
            kb.append(kb_h)
            qn.append(qn_h)
            kn.append(kn_h)
            vb.append(v * beta)
            kbeg.append(kb_h * eg)
            qdec.append(qn_h * eg)
            kdec.append(kn_h * jnp.exp(g_last - gc))
            gcol.append(gc)
            dlast.append(jnp.exp(g_last))

    hg = min(nh, MXU_TILE // chunk)
    ntg = nh // hg
    wt = hg * chunk
    kt = hg * hd
    sg_per_tg = kt // MXU_TILE
    tgs = range(nb * ntg)
    cat = lambda xs, t: jnp.concatenate(xs[t * hg:(t + 1) * hg], axis=1)
    bdc_ref, bdt_ref = bd_scratch if bd_scratch else (None, None)
    bd_c = _BlockDiag(chunk, hg, hd, nb * ntg, bdc_ref)
    bd_t = _BlockDiag(chunk, hg, chunk, nb * ntg, bdt_ref)
    state_mask = _block_mask(MXU_TILE, MXU_TILE, hd, hd)
    ri = lax.broadcasted_iota(jnp.int32, (chunk, wt), 0)
    cw = lax.broadcasted_iota(jnp.int32, (chunk, wt), 1)
    cj = cw & (chunk - 1)
    strict_w = ri > cj
    incl_w = ri >= cj
    ones_cc = jnp.ones((chunk, chunk), BF16)

    decay = []
    for t in tgs:
        gce = jnp.broadcast_to(gcol[t * hg], (chunk, wt))
        for j in range(1, hg):
            gce = jnp.where(cw >= j * chunk, jnp.broadcast_to(gcol[t * hg + j], (chunk, wt)), gce)
        hi, mid, lo = _split3(jnp.where(ri == cj, gce, 0.0))
        gcj = (jnp.dot(ones_cc, hi, preferred_element_type=F32)
               + (jnp.dot(ones_cc, mid, preferred_element_type=F32) + jnp.dot(ones_cc, lo, preferred_element_type=F32)))
        decay.append(jnp.exp(jnp.where(incl_w, gce - gcj, -jnp.inf)))

    lhs = [jnp.concatenate([cat(kb, t), cat(qn, t)], axis=0).astype(BF16) for t in tgs]
    mk = [lax.dot_general(lhs[t], bd_c(cat(kn, t), 0, t), (((1,), (1,)), ((), ())),
                          preferred_element_type=F32) for t in tgs]
    a_mat = [jnp.where(strict_w, mk[t][:chunk] * decay[t], 0.0) for t in tgs]
    attn = [jnp.where(incl_w, mk[t][chunk:] * decay[t], 0.0) for t in tgs]
    t_mat = _unit_lower_inverse_wide(a_mat, chunk, hg, bd_t)
    t_bf = [x.astype(BF16) for x in t_mat]
    u = [jnp.dot(t_bf[t], bd_c(cat(vb, t), 1, t), preferred_element_type=F32) for t in tgs]
    w = [jnp.dot(t_bf[t], bd_c(cat(kbeg, t), 2, t), preferred_element_type=F32) for t in tgs]

    v_new, qs = [], []
    for t in tgs:
        qd = cat(qdec, t)
        parts = []
        for q_ in range(sg_per_tg):
            sl = slice(q_ * MXU_TILE, (q_ + 1) * MXU_TILE)
            parts.append(_dot(jnp.concatenate([w[t][:, sl], qd[:, sl]], axis=0), s_sc[t * sg_per_tg + q_]))
        ws = parts[0] if sg_per_tg == 1 else jnp.concatenate(parts, axis=1)
        v_new.append(u[t] - ws[:chunk])
        qs.append(ws[chunk:])
    o = [qs[t] + jnp.dot(attn[t].astype(BF16), bd_c(v_new[t], 0, t), preferred_element_type=F32)
         for t in tgs]
    join = lambda xs, s_: xs[s_] if ntg == 1 else jnp.concatenate(xs[s_ * ntg:(s_ + 1) * ntg], axis=1)
    lane = lax.broadcasted_iota(jnp.int32, (1, MXU_TILE), 1)
    for s_ in range(nb):
        o_ref[s_] = join(o, s_)
        vn_all = join(v_new, s_)
        for g in range(nsg):
            sl = slice(g * MXU_TILE, (g + 1) * MXU_TILE)
            h0 = s_ * nh + g * sgh
            dl = jnp.broadcast_to(dlast[h0], (1, MXU_TILE))
            for j in range(1, sgh):
                dl = jnp.where(lane >= j * hd, jnp.broadcast_to(dlast[h0 + j], (1, MXU_TILE)), dl)
            upd = _dot_tn(jnp.concatenate(kdec[h0:h0 + sgh], axis=1), vn_all[:, sl])
            s_sc[s_ * nsg + g] = s_sc[s_ * nsg + g] * dl + jnp.where(state_mask, upd, 0.0)

    @pl.when(c == pl.num_programs(1) - 1)
    def _():
        if first_writer:
            sout_ref[...] = jnp.zeros_like(sout_ref)
        for s_ in range(nb):
            for h in range(nh):
                g, j = divmod(h, sgh)
                sout[s_, h] = s_sc[s_ * nsg + g, j * hd:(j + 1) * hd, j * hd:(j + 1) * hd]


def _gdn_rec(qkv, conv_init, ba, conv_w, alr, dbr, s0, s_acc, layer, *, chunk, t_valid, nb):
    bsz, tp, width = qkv.shape
    assert bsz % nb == 0 and tp % chunk == 0
    blk = (nb, GDN_HEADS, GDN_HEAD_DIM, GDN_HEAD_DIM)
    state = pl.BlockSpec((None,) + blk, lambda b, c: (layer, b, 0, 0, 0))
    first = s_acc is None
    out_state = pl.BlockSpec((s0.shape[0],) + blk, lambda b, c: (0, b, 0, 0, 0)) if first else state
    nc = tp // chunk
    cb = chunk // SUBLANES
    dv = GDN_HEADS * GDN_HEAD_DIM
    (conv_w, conv_w_spec), (alr, alr_spec), (dbr, dbr_spec) = (_whole(x, 2) for x in (conv_w, alr, dbr))
    return pl.pallas_call(
        functools.partial(_gdn_rec_body, chunk=chunk, t_valid=t_valid, nb=nb, layer=layer, first_writer=first),
        grid=(bsz // nb, nc),
        in_specs=[pl.BlockSpec((nb, chunk, GDN_QKV), lambda b, c: (b, c, 0)),
                  pl.BlockSpec((nb, SUBLANES, GDN_QKV), lambda b, c: (b, jnp.maximum(c * cb - 1, 0), 0)),
                  pl.BlockSpec((nb, SUBLANES, GDN_QKV), lambda b, c: (b, 0, 0)),
                  pl.BlockSpec((nb, chunk, LANES), lambda b, c: (b, c, 0)),
                  conv_w_spec, alr_spec, dbr_spec,
                  state] + ([] if first else [pl.BlockSpec(memory_space=pl.ANY)]),
        out_specs=[pl.BlockSpec((nb, chunk, dv), lambda b, c: (b, c, 0)), out_state],
        out_shape=[jax.ShapeDtypeStruct((bsz, tp, dv), F32), jax.ShapeDtypeStruct(s0.shape, F32)],
        input_output_aliases={} if first else {8: 1},
        scratch_shapes=[pltpu.VMEM((nb, chunk + SUBLANES, GDN_QKV), F32),
                        pltpu.VMEM((nb * dv // MXU_TILE, MXU_TILE, MXU_TILE), F32)]
        + _block_diag_scratch(chunk, nb, GDN_HEADS, GDN_HEAD_DIM),
        compiler_params=_params("parallel", "arbitrary"),
        name="gdn_rec",
    )(qkv, qkv, conv_init, ba, conv_w, alr, dbr, s0, *([] if first else [s_acc]))


def _rwkv_proj_body(*refs, has_vres, seq_len, tm):
    if has_vres:
        (x_ref, prev_ref, start_ref, vf_ref, lerp_ref, wr_ref, wk_ref, wv_ref, w0_ref, w1_ref, w2_ref, a0_ref,
         a1_ref, a2_ref, g1_ref, g2_ref, v0_ref, v1_ref, v2_ref, r_ref, k_ref, v_ref, lw_ref, a_ref, g_ref) = refs
    else:
        (x_ref, prev_ref, start_ref, lerp_ref, wr_ref, wk_ref, wv_ref, w0_ref, w1_ref, w2_ref, a0_ref, a1_ref,
         a2_ref, g1_ref, g2_ref, r_ref, k_ref, v_ref, lw_ref, a_ref, g_ref) = refs
    x = x_ref[...]
    rolled = pltpu.roll(x, 1, axis=0)
    rows = lax.broadcasted_iota(jnp.int32, (tm, 1), 0)
    if tm <= seq_len:
        i = pl.program_id(0)
        tiles_per_seq = seq_len // tm
        carried = start_ref[pl.ds(i // tiles_per_seq, 1), :]
        before = prev_ref[SUBLANES - 1:SUBLANES, :]
        first = jnp.where(i % tiles_per_seq == 0, carried, before)
        xp = jnp.where(rows == 0, first, rolled)
    else:
        xp = jnp.where((rows & (seq_len - 1)) == 0, start_ref[...], rolled)
    dx = xp - x

    def mix(s):
        return (x + dx * lerp_ref[s:s + 1, :]).astype(BF16)

    r_ref[...] = jnp.dot(mix(0), wr_ref[...], preferred_element_type=F32)
    k_ref[...] = jnp.dot(mix(1), wk_ref[...], preferred_element_type=F32)
    xv = mix(2)
    v = jnp.dot(xv, wv_ref[...], preferred_element_type=F32)
    if has_vres:
        gate = jax.nn.sigmoid(v0_ref[...] + _dot(jnp.dot(xv, v1_ref[...], preferred_element_type=F32), v2_ref[...]))
        v = v + (vf_ref[...] - v) * gate
    v_ref[...] = v
    dec = w0_ref[...] + _dot(jnp.tanh(jnp.dot(mix(3), w1_ref[...], preferred_element_type=F32)), w2_ref[...])
    lw_ref[...] = -math.exp(-0.5) * jax.nn.sigmoid(dec)
    a_ref[...] = jax.nn.sigmoid(a0_ref[...] + _dot(jnp.dot(mix(4), a1_ref[...], preferred_element_type=F32),
                                                   a2_ref[...]))
    g_ref[...] = _dot(jax.nn.sigmoid(jnp.dot(mix(5), g1_ref[...], preferred_element_type=F32)), g2_ref[...])


def _rwkv_proj(x, shift, v_first, lerp, wr, wk, wv, w0, w1, w2, a0, a1, a2, g1, g2, vres, *, seq_len, tm):
    n, d = x.shape
    has_vres = vres is not None
    row = pl.BlockSpec((tm, d), lambda i: (i, 0))
    full = lambda a: _whole(a, 1, pipeline_mode=pl.Buffered(1))
    tb = tm // SUBLANES
    if tm <= seq_len:
        assert seq_len % tm == 0
        start = full(shift)
    else:
        assert tm % seq_len == 0 and seq_len & (seq_len - 1) == 0
        start = (jnp.repeat(shift.arr[shift.idx], seq_len, axis=0), row)
    prev_spec = pl.BlockSpec((SUBLANES, d), lambda i: (jnp.maximum(i * tb - 1, 0), 0))
    ops = ([(x, row), (x, prev_spec), start] + ([(v_first, row)] if has_vres else [])
           + [full(a) for a in (lerp, wr, wk, wv, w0, w1, w2, a0, a1, a2, g1, g2) + (tuple(vres) if has_vres else ())])
    args = [op for op, _ in ops]
    in_specs = [spec for _, spec in ops]
    return pl.pallas_call(
        functools.partial(_rwkv_proj_body, has_vres=has_vres, seq_len=seq_len, tm=tm),
        grid=(n // tm,),
        in_specs=in_specs,
        out_specs=[row] * 6,
        out_shape=[jax.ShapeDtypeStruct((n, d), F32)] * 6,
        compiler_params=_params("parallel"),
        name="rwkv_proj_vres" if has_vres else "rwkv_proj",
    )(*args)


def _rwkv_rec_body(r_ref, k_ref, v_ref, lw_ref, a_ref, kk_ref, ka_ref, rk_ref, lnw_ref, lnb_ref, s0_ref,
                   *rest, chunk, t_valid, heads, nb, layer, first_writer):
    y_ref, sout_ref, s_sc, *bd_scratch = rest if first_writer else rest[1:]
    sout = sout_ref.at[layer] if first_writer else sout_ref
    c = pl.program_id(1)
    hd = RWKV_HEAD
    d = heads * hd
    gw = MXU_TILE
    sgh = gw // hd
    nsg = heads // sgh
    hg = min(heads, gw // chunk)
    ntg = heads // hg
    kt = hg * hd
    sg_per_tg = kt // gw

    @pl.when(c == 0)
    def _():
        s_sc[...] = jnp.zeros_like(s_sc)
        for q in range(nb):
            for h in range(heads):
                g, j = divmod(h, sgh)
                s_sc[q * nsg + g, j * hd:(j + 1) * hd, j * hd:(j + 1) * hd] = s0_ref[q, h]
        for ref in bd_scratch:
            ref[...] = jnp.zeros_like(ref)

    state_mask = _block_mask(gw, gw, hd, hd)
    ones_bd = state_mask.astype(BF16)

    def head_sums(xs):
        parts = []
        for x in xs:
            for g in range(nsg):
                parts.extend(_split2(x[:, g * gw:(g + 1) * gw]))
        sums = _shared_rhs_dots(parts, ones_bd)
        return [jnp.concatenate([sums[2 * (i * nsg + g)] + sums[2 * (i * nsg + g) + 1] for g in range(nsg)], axis=1)
                for i in range(len(xs))]

    incl, _ = _tri_incl(chunk)
    tri_bf = incl.astype(BF16)

    def load(q):
        r_all = r_ref[q]
        k_all = k_ref[q]
        v_all = v_ref[q]
        lw_all = lw_ref[q]
        a_all = a_ref[q]
        if t_valid < chunk:
            row_ok = lax.broadcasted_iota(jnp.int32, (chunk, 1), 0) < t_valid
            r_all = jnp.where(row_ok, r_all, 0.0)
            k_all = jnp.where(row_ok, k_all, 0.0)
            v_all = jnp.where(row_ok, v_all, 0.0)
            lw_all = jnp.where(row_ok, lw_all, 0.0)
        kx = k_all * kk_ref[...]
        kmod = k_all * (1.0 + (a_all - 1.0) * ka_ref[...])
        return dict(r=r_all, v=v_all, lw=lw_all, a=a_all, kx=kx, kmod=kmod)

    seqs = [load(q) for q in range(nb)]
    sums = head_sums([s["kx"] * s["kx"] for s in seqs] + [s["r"] * s["kmod"] * rk_ref[...] for s in seqs])

    def prep(q):
        s = seqs[q]
        gc = _cumsum_rows(tri_bf, s["lw"])
        kk = s["kx"] * lax.rsqrt(sums[q] + L2_EPS)
        bv = kk * s["a"]
        e_neg = jnp.exp(-gc)
        g_last = gc[chunk - 1:chunk, :]
        e_rem = jnp.exp(g_last - gc)
        return dict(v=s["v"], rk_sum=sums[nb + q], at=-kk * jnp.exp(gc - s["lw"]), rt=s["r"] * jnp.exp(gc),
                    bt=bv * e_neg, kt=s["kmod"] * e_neg, bp=bv * e_rem, kp=s["kmod"] * e_rem, dec=jnp.exp(g_last))

    seqs = [prep(q) for q in range(nb)]

    bdc_ref, bdt_ref = bd_scratch if bd_scratch else (None, None)
    bd_c = _BlockDiag(chunk, hg, hd, nb * ntg, bdc_ref)
    bd_t = _BlockDiag(chunk, hg, chunk, nb * ntg, bdt_ref)
    ri = lax.broadcasted_iota(jnp.int32, (chunk, hg * chunk), 0)
    cj = lax.broadcasted_iota(jnp.int32, (chunk, hg * chunk), 1) & (chunk - 1)
    strict_w = ri > cj
    incl_w = ri >= cj

    tgs = range(nb * ntg)
    lane = lambda name, t: seqs[t // ntg][name][:, (t % ntg) * kt:(t % ntg + 1) * kt]
    lhs = [jnp.concatenate([lane("at", t), lane("rt", t)], axis=0).astype(BF16) for t in tgs]
    mb = [lax.dot_general(lhs[t], bd_c(lane("bt", t), 0, t), (((1,), (1,)), ((), ())),
                          preferred_element_type=F32) for t in tgs]
    mk = [lax.dot_general(lhs[t], bd_c(lane("kt", t), 1, t), (((1,), (1,)), ((), ())),
                          preferred_element_type=F32) for t in tgs]
    a_ab = [jnp.where(strict_w, mb[t][:chunk], 0.0) for t in tgs]
    a_rb = [jnp.where(incl_w, mb[t][chunk:], 0.0) for t in tgs]
    a_ak = [jnp.where(strict_w, mk[t][:chunk], 0.0) for t in tgs]
    a_rk = [jnp.where(incl_w, mk[t][chunk:], 0.0) for t in tgs]
    t_mat = _unit_lower_inverse_wide([-x for x in a_ab], chunk, hg, bd_t)

    xs = []
    for t in tgs:
        parts = []
        for q in range(sg_per_tg):
            g = t * sg_per_tg + q
            parts.append(lax.dot_general(lhs[t][:, q * gw:(q + 1) * gw], s_sc[g].astype(BF16),
                                         (((1,), (1,)), ((), ())), preferred_element_type=F32))
        xs.append(parts[0] if len(parts) == 1 else jnp.concatenate(parts, axis=1))
    av = [_shared_rhs_dots([a_ak[t], a_rk[t]], bd_c(lane("v", t), 2, t)) for t in tgs]
    rhs_u = [xs[t][:chunk] + av[t][0] for t in tgs]
    u = [jnp.dot(t_mat[t].astype(BF16), bd_c(rhs_u[t], 0, t), preferred_element_type=F32) for t in tgs]
    y = [xs[t][chunk:] + jnp.dot(a_rb[t].astype(BF16), bd_c(u[t], 1, t), preferred_element_type=F32) + av[t][1]
         for t in tgs]

    join = lambda xs_, q: xs_[q] if ntg == 1 else jnp.concatenate(xs_[q * ntg:(q + 1) * ntg], axis=1)
    for q in range(nb):
        sq = seqs[q]
        u_all = join(u, q)
        for g in range(nsg):
            sl = slice(g * gw, (g + 1) * gw)
            upd = _dot_tn(jnp.concatenate([u_all[:, sl], sq["v"][:, sl]], axis=0),
                          jnp.concatenate([sq["bp"][:, sl], sq["kp"][:, sl]], axis=0))
            s_sc[q * nsg + g] = s_sc[q * nsg + g] * sq["dec"][:, sl] + jnp.where(state_mask, upd, 0.0)

    y_all = [join(y, q) for q in range(nb)]
    mean = head_sums(y_all)
    yc = [y_all[q] - mean[q] * (1.0 / hd) for q in range(nb)]
    var = head_sums([x * x for x in yc])
    for q in range(nb):
        yn = yc[q] * lax.rsqrt(var[q] * (1.0 / hd) + GN_EPS)
        y_ref[q] = yn * lnw_ref[...] + lnb_ref[...] + seqs[q]["rk_sum"] * seqs[q]["v"]

    @pl.when(c == pl.num_programs(1) - 1)
    def _():
        if first_writer:
            sout_ref[...] = jnp.zeros_like(sout_ref)
        for q in range(nb):
            for h in range(heads):
                g, j = divmod(h, sgh)
                sout[q, h] = s_sc[q * nsg + g, j * hd:(j + 1) * hd, j * hd:(j + 1) * hd]


def _rwkv_rec(r, k, v, lw, a, k_k, k_a, r_k, ln_w, ln_b, s0, s_acc, layer, *, chunk, t_valid, nb):
    bsz, tp, d = r.shape
    assert bsz % nb == 0 and tp % chunk == 0
    heads = d // RWKV_HEAD
    nc = tp // chunk
    seq = pl.BlockSpec((nb, chunk, d), lambda b, c: (b, c, 0))
    vecs = [_whole(x, 2) for x in (k_k, k_a, r_k, ln_w, ln_b)]
    blk = (nb, heads, RWKV_HEAD, RWKV_HEAD)
    st = pl.BlockSpec((None,) + blk, lambda b, c: (layer, b, 0, 0, 0))
    first = s_acc is None
    out_st = pl.BlockSpec((s0.shape[0],) + blk, lambda b, c: (0, b, 0, 0, 0)) if first else st
    return pl.pallas_call(
        functools.partial(_rwkv_rec_body, chunk=chunk, t_valid=t_valid, heads=heads, nb=nb, layer=layer,
                          first_writer=first),
        grid=(bsz // nb, nc),
        in_specs=[seq] * 5 + [spec for _, spec in vecs] + [st] + ([] if first else [pl.BlockSpec(memory_space=pl.ANY)]),
        out_specs=[seq, out_st],
        out_shape=[jax.ShapeDtypeStruct((bsz, tp, d), F32), jax.ShapeDtypeStruct(s0.shape, F32)],
        input_output_aliases={} if first else {11: 1},
        scratch_shapes=[pltpu.VMEM((nb * d // MXU_TILE, MXU_TILE, MXU_TILE), F32)]
        + _block_diag_scratch(chunk, nb, heads, RWKV_HEAD),
        compiler_params=_params("parallel", "arbitrary"),
        name="rwkv_rec",
    )(r, k, v, lw, a, *[op for op, _ in vecs], s0, *([] if first else [s_acc]))


def _pad_time(x, tp):
    t = x.shape[1]
    if t == tp:
        return x
    return jnp.pad(x, [(0, 0), (0, tp - t)] + [(0, 0)] * (x.ndim - 2))


def _tiles(n):
    return 1024 if n % 1024 == 0 else n


def _trunk(h, bsz, t, gdn_s, gdn_buf, rw_s, rw_shift, p, chunk, nb):
    n, d = h.shape
    depth = p["norm_g"].shape[0]
    tm = _tiles(n)
    tp = -(-t // chunk) * chunk
    out_gdn_buf, out_rw_shift = [], []
    out_gdn_s = out_rw_s = None
    v_first = None
    for i in range(depth):
        row = lambda s: _Layer(p["norm_g"], i, s)
        h, u = _ffn(h, row(0), row(1), row(2), p["ffn_w_in"], p["ffn_w_out"], i, 0,
                    emit_u=True, tm=min(tm, 512), tf=256, u_dtype=BF16 if i % 2 == 0 else F32)
        j = i // 2
        if i % 2 == 0:
            at = lambda name: _Layer(p[name], j)
            main, ba = _gdn_proj(u, at("gdn_w_main"), at("gdn_w_ba"), tm=min(tm, 512))
            main3 = main.reshape(bsz, t, main.shape[1])
            out_gdn_buf.append(main3[:, t - (GDN_CONV - 1):, :GDN_QKV])
            ba3 = _pad_time(ba.reshape(bsz, t, LANES), tp)
            conv_init = jnp.pad(gdn_buf[j], [(0, 0), (SUBLANES - (GDN_CONV - 1), 0), (0, 0)])
            qkv = main3 if tp == t else _pad_time(main3[:, :, :GDN_QKV], tp)
            o, out_gdn_s = _gdn_rec(qkv, conv_init, ba3, at("gdn_conv_w"), at("gdn_alr"), at("gdn_dbr"),
                                    gdn_s, out_gdn_s, j, chunk=chunk, t_valid=min(t, chunk), nb=nb)
            o2 = o[:, :t].reshape(n, o.shape[2])
            mixer, mixer_args = "gdn", (o2, main, at("gdn_o_norm"), at("gdn_w_out"), row(3))
        else:
            u3 = u.reshape(bsz, t, d)
            out_rw_shift.append(u3[:, -1])
            at = lambda name: _Layer(p["rwkv_" + name], j)
            vres = None if j == 0 else tuple(_Layer(p["rwkv_" + name], j - 1) for name in ("v0", "v1", "v2"))
            w_rkv = [_Layer(p["rwkv_w_rkv"], j, s) for s in range(3)]
            r, k, v, lw, a, g = _rwkv_proj(
                u, _Layer(rw_shift, j), v_first, at("lerp"), *w_rkv, at("w0"), at("w1"), at("w2"), at("a0"),
                at("a1"), at("a2"), at("g1"), at("g2"), vres, seq_len=t, tm=min(tm, 512))
            if v_first is None:
                v_first = v
            seq = lambda x: _pad_time(x.reshape(bsz, t, d), tp)
            y, out_rw_s = _rwkv_rec(seq(r), seq(k), seq(v), seq(lw), seq(a), at("k_k"), at("k_a"),
                                    at("r_k"), at("ln_w"), at("ln_b"), rw_s, out_rw_s, j,
                                    chunk=chunk, t_valid=min(t, chunk), nb=nb)
            mixer, mixer_args = "rwkv", (y[:, :t].reshape(n, d), g, at("w_o"), row(3))
        h = _ffn(h, row(4), row(5), row(5), p["ffn_w_in"], p["ffn_w_out"], i, 1,
                 emit_u=False, tm=min(tm, 512), tf=256, mixer=mixer, mixer_args=mixer_args)[0]
    return h.reshape(bsz, t, d), out_gdn_s, jnp.stack(out_gdn_buf), out_rw_s, jnp.stack(out_rw_shift)


def kernel(x_prompt, x_sample, state_gdn, state_gdn_conv, state_rwkv, state_rwkv_shift, norm_g, ffn_w_in, ffn_w_out, gdn_w_in, gdn_conv_w, gdn_a_log, gdn_dt_bias, gdn_o_norm, gdn_w_out, rwkv_lerp, rwkv_w_rkv, rwkv_w0, rwkv_w1, rwkv_w2, rwkv_a0, rwkv_a1, rwkv_a2, rwkv_v0, rwkv_v1, rwkv_v2, rwkv_g1, rwkv_g2, rwkv_k_k, rwkv_k_a, rwkv_r_k, rwkv_ln_w, rwkv_ln_b, rwkv_w_o):
    bsz, t, d = x_prompt.shape
    dbsz, dt, _ = x_sample.shape
    n_gdn = gdn_w_in.shape[0]
    n_rwkv = rwkv_w_rkv.shape[0]
    nh = GDN_HEADS
    n_main = GDN_QKV + nh * GDN_HEAD_DIM

    def lane_slot(x, offset):
        return jnp.pad(x, [(0, 0), (offset, LANES - offset - nh)])[:, None, :]

    row3 = lambda x: x[:, None, :]
    p = dict(
        norm_g=norm_g[:, :, None, :],
        ffn_w_in=ffn_w_in.astype(BF16),
        ffn_w_out=ffn_w_out.astype(BF16),
        gdn_w_main=gdn_w_in[:, :, :n_main].astype(BF16),
        gdn_w_ba=jnp.pad(gdn_w_in[:, :, n_main:], [(0, 0), (0, 0), (0, LANES - 2 * nh)]).astype(BF16),
        gdn_conv_w=gdn_conv_w,
        gdn_alr=lane_slot(gdn_a_log, nh), gdn_dbr=lane_slot(gdn_dt_bias, nh),
        gdn_o_norm=row3(gdn_o_norm),
        gdn_w_out=gdn_w_out.astype(BF16),
        rwkv_lerp=jnp.pad(rwkv_lerp, [(0, 0), (0, SUBLANES - rwkv_lerp.shape[1]), (0, 0)]),
        rwkv_w_rkv=rwkv_w_rkv.astype(BF16),
        rwkv_w0=row3(rwkv_w0), rwkv_w1=rwkv_w1.astype(BF16), rwkv_w2=rwkv_w2.astype(BF16),
        rwkv_a0=row3(rwkv_a0), rwkv_a1=rwkv_a1.astype(BF16), rwkv_a2=rwkv_a2.astype(BF16),
        rwkv_v0=row3(rwkv_v0), rwkv_v1=rwkv_v1.astype(BF16), rwkv_v2=rwkv_v2.astype(BF16),
        rwkv_g1=rwkv_g1.astype(BF16), rwkv_g2=rwkv_g2.astype(BF16),
        rwkv_k_k=row3(rwkv_k_k), rwkv_k_a=row3(rwkv_k_a),
        rwkv_r_k=rwkv_r_k.reshape(n_rwkv, 1, d),
        rwkv_ln_w=row3(rwkv_ln_w), rwkv_ln_b=row3(rwkv_ln_b),
        rwkv_w_o=rwkv_w_o.astype(BF16),
    )
    dtype = x_prompt.dtype
    zero_gdn = jnp.zeros((n_gdn, bsz) + state_gdn.shape[2:], dtype)
    zero_conv = jnp.zeros((n_gdn, bsz) + state_gdn_conv.shape[2:], dtype)
    zero_rwkv = jnp.zeros((n_rwkv, bsz) + state_rwkv.shape[2:], dtype)
    zero_shift = jnp.zeros((n_rwkv, bsz, d), dtype)
    prompt = _trunk(x_prompt.reshape(bsz * t, d), bsz, t, zero_gdn, zero_conv, zero_rwkv, zero_shift, p,
                    chunk=64, nb=4)
    sample = _trunk(x_sample.reshape(dbsz * dt, d), dbsz, dt, state_gdn, state_gdn_conv, state_rwkv,
                    state_rwkv_shift, p, chunk=SUBLANES, nb=8)
    return (prompt[0], sample[0]) + prompt[1:] + sample[1:]
```
